```python
import math
import jax, jax.numpy as jnp
from jax import lax
import numpy as np

D_MODEL = 1024
BATCH = 2
SEQ = 8192
DEPTH = 1

N_POOL_GROUPS = 4
POOL_WINDOWS = (2, 4, 8, 16)
D_POOL = D_MODEL // 2
POOL_GROUP_DIM = D_POOL // N_POOL_GROUPS
HEAD_DIM = 64
D_ATTN = D_MODEL - D_POOL
N_HEADS = D_ATTN // HEAD_DIM
D_MIX = D_POOL + D_ATTN
D_IN = D_POOL + 3 * D_ATTN
D_FF = ((8 * D_MODEL // 3 + 255) // 256) * 256
CONV_WIDTH = 3
Q_BLOCK = 128
N_MOD = 6
EPS = 1e-6

kernel_name = "hybrid_pool_stickbreaking_convffn_adaln"


def rmsnorm(x, g):
    xf = x.astype(jnp.float32)
    y = xf * lax.rsqrt(jnp.mean(xf * xf, axis=-1, keepdims=True) + EPS)
    return (y * g.astype(jnp.float32)).astype(x.dtype)


def modulate(h, shift, scale):
    return h * (1.0 + scale[:, None, :]) + shift[:, None, :]


def pool_mixer(u, pool_w, pool_b, pool_scale):
    b, s, _ = u.shape
    uf = u.astype(jnp.float32).reshape(b, s, N_POOL_GROUPS, POOL_GROUP_DIM)
    cs = jnp.concatenate([jnp.zeros_like(uf[:, :1]), jnp.cumsum(uf, axis=1)], axis=1)
    t1 = jnp.arange(1, s + 1)
    outs = []
    for g, w in enumerate(POOL_WINDOWS):
        cg = cs[:, :, g]
        lo = jnp.maximum(t1 - w, 0)
        win_sum = cg[:, 1:] - cg[:, lo]
        count = jnp.minimum(t1, w).astype(jnp.float32)
        outs.append(win_sum / count[None, :, None] - uf[:, :, g])
    pooled = jnp.stack(outs, axis=2)
    y = jnp.einsum('bsgc,gcd->bsgd', pooled, pool_w.astype(jnp.float32)) + pool_b.astype(jnp.float32)
    y = y.reshape(b, s, D_POOL) * pool_scale.astype(jnp.float32)
    return y.astype(u.dtype)


def stick_breaking_attention(q, k, v):
    b, h, s, d = q.shape
    nb = s // Q_BLOCK
    qb = q.reshape(b, h, nb, Q_BLOCK, d).transpose(2, 0, 1, 3, 4)
    k_pos = jnp.arange(s)
    inv_sqrt_d = 1.0 / math.sqrt(d)

    def one_block(args):
        q_blk, blk = args
        q_pos = blk * Q_BLOCK + jnp.arange(Q_BLOCK)
        causal = k_pos[None, :] < q_pos[:, None]
        z = jnp.einsum('bhqd,bhkd->bhqk', q_blk, k) * inv_sqrt_d
        log_beta = jax.nn.log_sigmoid(z)
        log_1m = jnp.where(causal, jax.nn.log_sigmoid(-z), 0.0)
        later = lax.cumsum(log_1m, axis=3, reverse=True) - log_1m
        a = jnp.exp(jnp.where(causal, log_beta + later, -jnp.inf))
        return jnp.einsum('bhqk,bhkd->bhqd', a, v)

    o = lax.map(one_block, (qb, jnp.arange(nb)))
    return o.transpose(1, 2, 0, 3, 4).reshape(b, h, s, d)


def causal_depthwise_conv(u, w, bias):
    ch = u.shape[-1]
    y = lax.conv_general_dilated(
        u, w[:, None, :].astype(u.dtype), window_strides=(1,),
        padding=[(CONV_WIDTH - 1, 0)], dimension_numbers=('NWC', 'WIO', 'NWC'),
        feature_group_count=ch)
    return y + bias.astype(u.dtype)


def setup_inputs(seed: int = 0) -> dict:
    key = jax.random.key(seed)
    ks = jax.random.split(key, 20)
    f32 = jnp.float32
    n = lambda k, shape, s: jax.random.normal(k, shape, f32) * s
    return {
        "x": n(ks[0], (BATCH, SEQ, D_MODEL), 1.0),
        "c": n(ks[1], (BATCH, D_MODEL), 1.0),
        "ada_w": n(ks[2], (DEPTH, D_MODEL, N_MOD * D_MODEL), 0.5 * D_MODEL ** -0.5),
        "ada_b": n(ks[3], (DEPTH, N_MOD * D_MODEL), 0.02),
        "norm1_g": 1.0 + n(ks[4], (DEPTH, D_MODEL), 0.05),
        "w_in": n(ks[5], (DEPTH, D_MODEL, D_IN), D_MODEL ** -0.5),
        "pool_w": n(ks[6], (DEPTH, N_POOL_GROUPS, POOL_GROUP_DIM, POOL_GROUP_DIM), POOL_GROUP_DIM ** -0.5),
        "pool_b": n(ks[7], (DEPTH, N_POOL_GROUPS, POOL_GROUP_DIM), 0.02),
        "pool_scale": 1.0 + n(ks[8], (DEPTH, D_POOL), 0.1),
        "q_norm_g": 1.0 + n(ks[9], (DEPTH, HEAD_DIM), 0.05),
        "k_norm_g": 1.0 + n(ks[10], (DEPTH, HEAD_DIM), 0.05),
        "attn_out_g": 1.0 + n(ks[11], (DEPTH, N_HEADS, HEAD_DIM), 0.05),
        "w_out": n(ks[12], (DEPTH, D_MIX, D_MODEL), D_MIX ** -0.5),
        "norm2_g": 1.0 + n(ks[13], (DEPTH, D_MODEL), 0.05),
        "w_up": n(ks[14], (DEPTH, D_MODEL, 2 * D_FF), D_MODEL ** -0.5),
        "conv_w": n(ks[15], (DEPTH, CONV_WIDTH, 2 * D_FF), CONV_WIDTH ** -0.5),
        "conv_b": n(ks[16], (DEPTH, 2 * D_FF), 0.02),
        "w_down": n(ks[17], (DEPTH, D_FF, D_MODEL), D_FF ** -0.5),
    }


def reference(x, c, ada_w, ada_b, norm1_g, w_in, pool_w, pool_b, pool_scale, q_norm_g, k_norm_g,
              attn_out_g, w_out, norm2_g, w_up, conv_w, conv_b, w_down):
    b, s, _ = x.shape
    c_act = jax.nn.silu(c)
    for l in range(DEPTH):
        mod = jnp.einsum('bd,de->be', c_act, ada_w[l]) + ada_b[l]
        shift1, scale1, gate1, shift2, scale2, gate2 = jnp.split(mod, N_MOD, axis=-1)

        h = modulate(rmsnorm(x, norm1_g[l]), shift1, scale1)
        proj = jnp.einsum('bsd,de->bse', h, w_in[l])
        u, q, k, v = jnp.split(proj, [D_POOL, D_POOL + D_ATTN, D_POOL + 2 * D_ATTN], axis=-1)

        y_pool = pool_mixer(u, pool_w[l], pool_b[l], pool_scale[l])

        to_heads = lambda t: t.reshape(b, s, N_HEADS, HEAD_DIM).transpose(0, 2, 1, 3)
        qh = rmsnorm(to_heads(q), q_norm_g[l]).astype(jnp.float32)
        kh = rmsnorm(to_heads(k), k_norm_g[l]).astype(jnp.float32)
        vh = to_heads(v).astype(jnp.float32)
        o = stick_breaking_attention(qh, kh, vh)
        o = o.transpose(0, 2, 1, 3).astype(x.dtype)
        o = rmsnorm(o, attn_out_g[l]).reshape(b, s, D_ATTN)

        mix = jnp.concatenate([y_pool, o], axis=-1)
        x = x + gate1[:, None, :] * jnp.einsum('bse,ed->bsd', mix, w_out[l])

        h2 = modulate(rmsnorm(x, norm2_g[l]), shift2, scale2)
        up = jnp.einsum('bsd,df->bsf', h2, w_up[l])
        up = causal_depthwise_conv(up, conv_w[l], conv_b[l])
        gate, val = jnp.split(up, 2, axis=-1)
        ffn = jnp.einsum('bsf,fd->bsd', jax.nn.silu(gate) * val, w_down[l])
        x = x + gate2[:, None, :] * ffn
    return x
```

```python
import functools
import math

import jax
import jax.numpy as jnp
from jax import lax
from jax.experimental import pallas as pl
from jax.experimental.pallas import tpu as pltpu

EPS = 1e-6
HEAD_DIM = 64
N_POOL_GROUPS = 4
POOL_WINDOWS = (2, 4, 8, 16)
POOL_HALO = 16
CONV_WIDTH = 3
CONV_HALO = 8
LANES = 128
Q_BLOCK = 128
K_BLOCK = 128
FF_CHUNK = 256
LOG_WEIGHT_ZERO = -104.7
VMEM_LIMIT = 48 * 1024 * 1024

bf16 = jnp.bfloat16
f32 = jnp.float32


def _const_spec(shape):
    nd = len(shape)
    return pl.BlockSpec(shape, lambda *_: (0,) * nd, pipeline_mode=pl.Buffered(1))


def _adaln_kernel(ct_ref, w_ref, b_ref, out_ref):
    ct = ct_ref[...]
    act = ct / (1.0 + jnp.exp(-ct))
    w = w_ref[...]
    rows = []
    for b in range(ct.shape[1]):
        rows.append(jnp.sum(act[:, b:b + 1] * w, axis=0, keepdims=True))
    out_ref[...] = jnp.concatenate(rows, axis=0) + b_ref[...]


def _adaln(c, ada_w, ada_b, tn=1536):
    bsz, d = c.shape
    n = ada_w.shape[1]
    return pl.pallas_call(
        _adaln_kernel,
        grid=(n // tn,),
        in_specs=[
            pl.BlockSpec((d, bsz), lambda j: (0, 0)),
            pl.BlockSpec((d, tn), lambda j: (0, j)),
            pl.BlockSpec((1, tn), lambda j: (0, j)),
        ],
        out_specs=pl.BlockSpec((bsz, tn), lambda j: (0, j)),
        out_shape=jax.ShapeDtypeStruct((bsz, n), f32),
        name="adaln",
        compiler_params=pltpu.CompilerParams(dimension_semantics=("arbitrary",)),
    )(c.T, ada_w, ada_b.reshape(1, n))


def _inproj_kernel(x_ref, mod_ref, g1_ref, w_in_ref, pool_w_ref, pool_b_ref, pool_s_ref,
                   qg_ref, kg_ref, headsum_ref,
                   yp_ref, q_ref, k_ref, v_ref, halo_ref, *, d_pool, d_attn):
    j = pl.program_id(1)
    tm = x_ref.shape[1]

    x = x_ref[0]
    shift = mod_ref[0, 0:1, :]
    scale = mod_ref[0, 1:2, :]
    inv_rms = lax.rsqrt(jnp.mean(x * x, axis=-1, keepdims=True) + EPS)
    h = (x * inv_rms) * (g1_ref[...] * (1.0 + scale)) + shift
    hb = h.astype(bf16)

    u = jnp.dot(hb, w_in_ref[:, 0:d_pool], preferred_element_type=f32)

    @pl.when(j == 0)
    def _():
        halo_ref[...] = jnp.zeros_like(halo_ref)

    ext = jnp.concatenate([halo_ref[...], u], axis=0)
    halo_ref[...] = u[tm - POOL_HALO:, :]
    t1 = (j * tm + 1 + lax.broadcasted_iota(jnp.int32, (tm, 1), 0))
    ys = []
    for g, w in enumerate(POOL_WINDOWS):
        cols = slice(g * LANES, (g + 1) * LANES)
        s = ext[:, cols]
        span = 1
        while span < w:
            s = s + pltpu.roll(s, span, axis=0)
            span *= 2
        count = jnp.minimum(t1, w).astype(f32)
        pooled = s[POOL_HALO:, :] / count - u[:, cols]
        y = jnp.dot(pooled.astype(bf16), pool_w_ref[g], preferred_element_type=f32)
        ys.append((y + pool_b_ref[:, cols]) * pool_s_ref[:, cols])
    yp_ref[0] = jnp.concatenate(ys, axis=-1).astype(yp_ref.dtype)

    def head_norm(cols, gain, post):
        t = jnp.dot(hb, w_in_ref[:, cols], preferred_element_type=f32)
        ssq = jnp.dot((t * t).astype(bf16), headsum_ref[...], preferred_element_type=f32)
        return (t * lax.rsqrt(ssq * (1.0 / HEAD_DIM) + EPS)) * (gain * post)

    q_ref[0] = head_norm(slice(d_pool, d_pool + d_attn), qg_ref[...],
                         1.0 / math.sqrt(HEAD_DIM)).astype(q_ref.dtype)
    k_ref[0] = head_norm(slice(d_pool + d_attn, d_pool + 2 * d_attn), kg_ref[...],
                         1.0).astype(k_ref.dtype)
    v_ref[0] = jnp.dot(hb, w_in_ref[:, d_pool + 2 * d_attn:],
                       preferred_element_type=f32).astype(v_ref.dtype)


def _inproj(x, mod, g1, w_in, pool_w, pool_b, pool_s, qg, kg, tm=512):
    bsz, s, d = x.shape
    d_pool = pool_b.shape[-1]
    d_attn = (w_in.shape[1] - d_pool) // 3
    n_heads = d_attn // HEAD_DIM
    headsum = jnp.kron(jnp.eye(n_heads, dtype=f32), jnp.ones((HEAD_DIM, HEAD_DIM), f32)).astype(bf16)
    tile = lambda width: pl.BlockSpec((1, tm, width), lambda b, j: (b, j, 0))
    out = jax.ShapeDtypeStruct((bsz, s, d_attn), bf16)
    return pl.pallas_call(
        functools.partial(_inproj_kernel, d_pool=d_pool, d_attn=d_attn),
        grid=(bsz, s // tm),
        in_specs=[
            tile(d),
            pl.BlockSpec((1,) + mod.shape[1:], lambda b, j: (b, 0, 0)),
            _const_spec(g1.shape),
            _const_spec(w_in.shape),
            _const_spec(pool_w.shape),
            _const_spec(pool_b.shape),
            _const_spec(pool_s.shape),
            _const_spec(qg.shape),
            _const_spec(kg.shape),
            _const_spec(headsum.shape),
        ],
        out_specs=[tile(d_pool), tile(d_attn), tile(d_attn), tile(d_attn)],
        out_shape=[jax.ShapeDtypeStruct((bsz, s, d_pool), bf16), out, out, out],
        scratch_shapes=[pltpu.VMEM((POOL_HALO, d_pool), f32)],
        name="inproj",
        compiler_params=pltpu.CompilerParams(
            dimension_semantics=("arbitrary", "arbitrary"), vmem_limit_bytes=VMEM_LIMIT),
    )(x, mod, g1, w_in, pool_w, pool_b, pool_s, qg, kg, headsum)


def _attn_kernel(q_ref, k_ref, v_ref, g_ref, o_ref):
    i = pl.program_id(2)
    lane = lax.broadcasted_iota(jnp.int32, (1, LANES), 1)
    head0 = lane < HEAD_DIM
    q = q_ref[0]
    qs = (jnp.where(head0, q, jnp.zeros_like(q)), jnp.where(head0, jnp.zeros_like(q), q))

    row = lax.broadcasted_iota(jnp.int32, (Q_BLOCK, K_BLOCK), 0)
    col = lax.broadcasted_iota(jnp.int32, (Q_BLOCK, K_BLOCK), 1)
    causal = col < row
    after = (row > col).astype(bf16)

    def key_block(jb, diag, carry):
        kb = k_ref[0, pl.ds(pl.multiple_of(jb * K_BLOCK, K_BLOCK), K_BLOCK), :]
        vb = v_ref[0, pl.ds(pl.multiple_of(jb * K_BLOCK, K_BLOCK), K_BLOCK), :]
        out = []
        for qh, (run, acc) in zip(qs, carry):
            z = lax.dot_general(qh, kb, (((1,), (1,)), ((), ())), preferred_element_type=f32)
            softplus = jnp.log(1.0 + jnp.exp(-jnp.abs(z)))
            log_beta = jnp.minimum(z, 0.0) - softplus
            log_1m = log_beta - z
            if diag:
                log_1m = jnp.where(causal, log_1m, 0.0)
            hi = log_1m.astype(bf16)
            lo = (log_1m - hi.astype(f32)).astype(bf16)
            later = (jnp.dot(hi, after, preferred_element_type=f32)
                     + jnp.dot(lo, after, preferred_element_type=f32))
            a = jnp.exp(log_beta + later + run)
            if diag:
                a = jnp.where(causal, a, 0.0)
            acc = acc + jnp.dot(a.astype(bf16), vb, preferred_element_type=f32)
            run = run + jnp.sum(log_1m, axis=1, keepdims=True)
            out.append((run, acc))
        return tuple(out)

    zero = (jnp.zeros((Q_BLOCK, 1), f32), jnp.zeros((Q_BLOCK, LANES), f32))
    carry = key_block(i, True, (zero, zero))

    def cond(state):
        jb, ((run0, _), (run1, _)) = state
        return jnp.logical_and(jb >= 0, jnp.max(jnp.maximum(run0, run1)) > LOG_WEIGHT_ZERO)

    def body(state):
        jb, carry = state
        return jb - 1, key_block(jb, False, carry)

    _, ((_, acc0), (_, acc1)) = lax.while_loop(cond, body, (i - 1, carry))

    o = jnp.where(head0, acc0, acc1)
    o2 = o * o
    ssq0 = jnp.sum(jnp.where(head0, o2, 0.0), axis=-1, keepdims=True)
    ssq1 = jnp.sum(jnp.where(head0, 0.0, o2), axis=-1, keepdims=True)
    ms = jnp.where(head0, ssq0, ssq1) * (1.0 / HEAD_DIM)
    o_ref[0] = (o * lax.rsqrt(ms + EPS) * g_ref[...]).astype(o_ref.dtype)


def _attention(q, k, v, out_g):
    bsz, s, d_attn = q.shape
    n_cols = d_attn // LANES
    qspec = pl.BlockSpec((1, Q_BLOCK, LANES), lambda b, p, i: (b, i, p))
    kvspec = pl.BlockSpec((1, s, LANES), lambda b, p, i: (b, 0, p))
    return pl.pallas_call(
        _attn_kernel,
        grid=(bsz, n_cols, s // Q_BLOCK),
        in_specs=[qspec, kvspec, kvspec, pl.BlockSpec((1, LANES), lambda b, p, i: (0, p))],
        out_specs=qspec,
        out_shape=jax.ShapeDtypeStruct((bsz, s, d_attn), bf16),
        name="stickbreak_attn",
        compiler_params=pltpu.CompilerParams(
            dimension_semantics=("arbitrary", "arbitrary", "arbitrary")),
    )(q, k, v, out_g)


def _ffn_kernel(x_ref, yp_ref, o_ref, mod_ref, g2_ref, w_out_ref, w_up_ref, cw_ref, cb_ref, w_down_ref,
                out_ref, halo_ref, act_ref, *, d_pool, d_ff):
    j = pl.program_id(1)
    tm = x_ref.shape[1]
    gate1 = mod_ref[0, 2:3, :]
    shift = mod_ref[0, 3:4, :]
    scale = mod_ref[0, 4:5, :]
    gate2 = mod_ref[0, 5:6, :]

    mixed = (jnp.dot(yp_ref[0], w_out_ref[0:d_pool, :], preferred_element_type=f32)
             + jnp.dot(o_ref[0], w_out_ref[d_pool:, :], preferred_element_type=f32))
    x1 = x_ref[0] + gate1 * mixed
    inv_rms = lax.rsqrt(jnp.mean(x1 * x1, axis=-1, keepdims=True) + EPS)
    hb = ((x1 * inv_rms) * (g2_ref[...] * (1.0 + scale)) + shift).astype(bf16)

    @pl.when(j == 0)
    def _():
        halo_ref[...] = jnp.zeros_like(halo_ref)

    def conv(c, cols):
        up = jnp.dot(hb, w_up_ref[:, cols], preferred_element_type=f32)
        ext = jnp.concatenate([halo_ref[c], up], axis=0)
        halo_ref[c] = up[tm - CONV_HALO:, :]
        prev1 = pltpu.roll(ext, 1, axis=0)[CONV_HALO:, :]
        prev2 = pltpu.roll(ext, 2, axis=0)[CONV_HALO:, :]
        return (cw_ref[2:3, cols] * up + cw_ref[1:2, cols] * prev1
                + cw_ref[0:1, cols] * prev2 + cb_ref[:, cols])

    n_chunks = d_ff // FF_CHUNK
    for c in range(n_chunks):
        gate = conv(c, slice(c * FF_CHUNK, (c + 1) * FF_CHUNK))
        val = conv(n_chunks + c, slice(d_ff + c * FF_CHUNK, d_ff + (c + 1) * FF_CHUNK))
        act = gate / (1.0 + jnp.exp(-gate)) * val
        act_ref[:, c * FF_CHUNK:(c + 1) * FF_CHUNK] = act.astype(bf16)

    ffn = jnp.dot(act_ref[...], w_down_ref[...], preferred_element_type=f32)
    out_ref[0] = x1 + gate2 * ffn


def _ffn(x, yp, o, mod, g2, w_out, w_up, conv_w, conv_b, w_down, tm=512):
    bsz, s, d = x.shape
    d_pool = yp.shape[-1]
    d_ff = w_down.shape[0]
    tile = lambda width: pl.BlockSpec((1, tm, width), lambda b, j: (b, j, 0))
    return pl.pallas_call(
        functools.partial(_ffn_kernel, d_pool=d_pool, d_ff=d_ff),
        grid=(bsz, s // tm),
        in_specs=[
            tile(d), tile(d_pool), tile(o.shape[-1]),
            pl.BlockSpec((1,) + mod.shape[1:], lambda b, j: (b, 0, 0)),
            _const_spec(g2.shape),
            _const_spec(w_out.shape),
            _const_spec(w_up.shape),
            _const_spec(conv_w.shape),
            _const_spec(conv_b.shape),
            _const_spec(w_down.shape),
        ],
        out_specs=tile(d),
        out_shape=jax.ShapeDtypeStruct((bsz, s, d), x.dtype),
        scratch_shapes=[
            pltpu.VMEM((2 * d_ff // FF_CHUNK, CONV_HALO, FF_CHUNK), f32),
            pltpu.VMEM((tm, d_ff), bf16),
        ],
        name="outproj_convffn",
        compiler_params=pltpu.CompilerParams(
            dimension_semantics=("arbitrary", "arbitrary"), vmem_limit_bytes=VMEM_LIMIT),
    )(x, yp, o, mod, g2, w_out, w_up, conv_w, conv_b, w_down)


def kernel(x, c, ada_w, ada_b, norm1_g, w_in, pool_w, pool_b, pool_scale, q_norm_g, k_norm_g, attn_out_g,
           w_out, norm2_g, w_up, conv_w, conv_b, w_down):
    depth = ada_w.shape[0]
    d = x.shape[-1]
    n_heads = attn_out_g.shape[1]
    row = lambda a: a.reshape(1, -1)
    for l in range(depth):
        mod = _adaln(c, ada_w[l], ada_b[l]).reshape(c.shape[0], -1, d)
        yp, q, k, v = _inproj(
            x, mod, row(norm1_g[l]), w_in[l].astype(bf16), pool_w[l].astype(bf16), row(pool_b[l]),
            row(pool_scale[l]), row(jnp.tile(q_norm_g[l], n_heads)), row(jnp.tile(k_norm_g[l], n_heads)))
        o = _attention(q, k, v, row(attn_out_g[l]))
        x = _ffn(x, yp, o, mod, row(norm2_g[l]), w_out[l].astype(bf16), w_up[l].astype(bf16),
                 conv_w[l], row(conv_b[l]), w_down[l].astype(bf16))
    return x
```

```python
import functools
import math

import jax
import jax.numpy as jnp
from jax import lax
from jax.experimental import pallas as pl
from jax.experimental.pallas import tpu as pltpu

EPS = 1e-6
HEAD_DIM = 64
N_POOL_GROUPS = 4
POOL_WINDOWS = (2, 4, 8, 16)
POOL_HALO = 16
CONV_WIDTH = 3
CONV_HALO = 8
LANES = 128
Q_BLOCK = 128
K_WIDE = 256
FF_CHUNK = 256
LOG_WEIGHT_ZERO = -104.7
VMEM_LIMIT = 48 * 1024 * 1024

bf16 = jnp.bfloat16
f32 = jnp.float32


def _const_spec(shape):
    nd = len(shape)
    return pl.BlockSpec(shape, lambda *_: (0,) * nd, pipeline_mode=pl.Buffered(1))


def _adaln_kernel(ct_ref, w_ref, b_ref, out_ref):
    ct = ct_ref[...]
    act = ct / (1.0 + jnp.exp(-ct))
    w = w_ref[...]
    rows = []
    for b in range(ct.shape[1]):
        rows.append(jnp.sum(act[:, b:b + 1] * w, axis=0, keepdims=True))
    out_ref[...] = jnp.concatenate(rows, axis=0) + b_ref[...]


def _adaln(c, ada_w, ada_b, tn=1536):
    bsz, d = c.shape
    n = ada_w.shape[1]
    return pl.pallas_call(
        _adaln_kernel,
        grid=(n // tn,),
        in_specs=[
            pl.BlockSpec((d, bsz), lambda j: (0, 0)),
            pl.BlockSpec((d, tn), lambda j: (0, j)),
            pl.BlockSpec((1, tn), lambda j: (0, j)),
        ],
        out_specs=pl.BlockSpec((bsz, tn), lambda j: (0, j)),
        out_shape=jax.ShapeDtypeStruct((bsz, n), f32),
        name="adaln",
        compiler_params=pltpu.CompilerParams(dimension_semantics=("arbitrary",)),
    )(c.T, ada_w, ada_b.reshape(1, n))


def _inproj_kernel(x_ref, mod_ref, g1_ref, w_in_ref, pool_w_ref, pool_b_ref, pool_s_ref,
                   qg_ref, kg_ref, headsum_ref,
                   yp_ref, q_ref, k_ref, v_ref, halo_ref, *, d_pool, d_attn):
    j = pl.program_id(1)
    tm = x_ref.shape[1]

    x = x_ref[0]
    shift = mod_ref[0, 0:1, :]
    scale = mod_ref[0, 1:2, :]
    inv_rms = lax.rsqrt(jnp.mean(x * x, axis=-1, keepdims=True) + EPS)
    h = (x * inv_rms) * (g1_ref[...] * (1.0 + scale)) + shift
    hb = h.astype(bf16)

    u = jnp.dot(hb, w_in_ref[:, 0:d_pool], preferred_element_type=f32)

    @pl.when(j == 0)
    def _():
        halo_ref[...] = jnp.zeros_like(halo_ref)

    ext = jnp.concatenate([halo_ref[...], u], axis=0)
    halo_ref[...] = u[tm - POOL_HALO:, :]
    t1 = (j * tm + 1 + lax.broadcasted_iota(jnp.int32, (tm, 1), 0))
    ys = []
    for g, w in enumerate(POOL_WINDOWS):
        cols = slice(g * LANES, (g + 1) * LANES)
        s = ext[:, cols]
        span = 1
        while span < w:
            s = s + pltpu.roll(s, span, axis=0)
            span *= 2
        count = jnp.minimum(t1, w).astype(f32)
        pooled = s[POOL_HALO:, :] / count - u[:, cols]
        y = jnp.dot(pooled.astype(bf16), pool_w_ref[g], preferred_element_type=f32)
        ys.append((y + pool_b_ref[:, cols]) * pool_s_ref[:, cols])
    yp_ref[0] = jnp.concatenate(ys, axis=-1).astype(yp_ref.dtype)

    def head_norm(cols, gain, post):
        t = jnp.dot(hb, w_in_ref[:, cols], preferred_element_type=f32)
        ssq = jnp.dot((t * t).astype(bf16), headsum_ref[...], preferred_element_type=f32)
        return (t * lax.rsqrt(ssq * (1.0 / HEAD_DIM) + EPS)) * (gain * post)

    q_ref[0] = head_norm(slice(d_pool, d_pool + d_attn), qg_ref[...],
                         1.0 / math.sqrt(HEAD_DIM)).astype(q_ref.dtype)
    k_ref[0] = head_norm(slice(d_pool + d_attn, d_pool + 2 * d_attn), kg_ref[...],
                         1.0).astype(k_ref.dtype)
    v_ref[0] = jnp.dot(hb, w_in_ref[:, d_pool + 2 * d_attn:],
                       preferred_element_type=f32).astype(v_ref.dtype)


def _inproj(x, mod, g1, w_in, pool_w, pool_b, pool_s, qg, kg, tm=512):
    bsz, s, d = x.shape
    d_pool = pool_b.shape[-1]
    d_attn = (w_in.shape[1] - d_pool) // 3
    n_heads = d_attn // HEAD_DIM
    headsum = jnp.kron(jnp.eye(n_heads, dtype=f32), jnp.ones((HEAD_DIM, HEAD_DIM), f32)).astype(bf16)
    tile = lambda width: pl.BlockSpec((1, tm, width), lambda b, j: (b, j, 0))
    out = jax.ShapeDtypeStruct((bsz, s, d_attn), bf16)
    return pl.pallas_call(
        functools.partial(_inproj_kernel, d_pool=d_pool, d_attn=d_attn),
        grid=(bsz, s // tm),
        in_specs=[
            tile(d),
            pl.BlockSpec((1,) + mod.shape[1:], lambda b, j: (b, 0, 0)),
            _const_spec(g1.shape),
            _const_spec(w_in.shape),
            _const_spec(pool_w.shape),
            _const_spec(pool_b.shape),
            _const_spec(pool_s.shape),
            _const_spec(qg.shape),
            _const_spec(kg.shape),
            _const_spec(headsum.shape),
        ],
        out_specs=[tile(d_pool), tile(d_attn), tile(d_attn), tile(d_attn)],
        out_shape=[jax.ShapeDtypeStruct((bsz, s, d_pool), bf16), out, out, out],
        scratch_shapes=[pltpu.VMEM((POOL_HALO, d_pool), f32)],
        name="inproj",
        compiler_params=pltpu.CompilerParams(
            dimension_semantics=("arbitrary", "arbitrary"), vmem_limit_bytes=VMEM_LIMIT),
    )(x, mod, g1, w_in, pool_w, pool_b, pool_s, qg, kg, headsum)


def _attn_kernel(q_ref, k_ref, v_ref, g_ref, after_ref, o_ref, run_ref, acc_ref):
    i = pl.program_id(2)
    n_cols = q_ref.shape[2] // LANES
    lane = lax.broadcasted_iota(jnp.int32, (1, LANES), 1)
    head0 = lane < HEAD_DIM

    def scores(c, q2, start, width):
        kb = k_ref[0, pl.ds(start, width), c * LANES:(c + 1) * LANES]
        return lax.dot_general(q2, kb, (((1,), (1,)), ((), ())), preferred_element_type=f32)

    def log_terms(z, diag):
        width = z.shape[1]
        softplus = jnp.log(1.0 + jnp.exp(-jnp.abs(z)))
        log_beta = jnp.minimum(z, 0.0) - softplus
        log_1m = log_beta - z
        causal = None
        if diag:
            row = lax.broadcasted_iota(jnp.int32, z.shape, 0) & (Q_BLOCK - 1)
            causal = lax.broadcasted_iota(jnp.int32, z.shape, 1) < row
            log_1m = jnp.where(causal, log_1m, 0.0)
        hi = log_1m.astype(bf16)
        lo = (log_1m - hi.astype(f32)).astype(bf16)
        both = jnp.dot(jnp.concatenate([hi, lo], axis=0), after_ref[0:width, 0:width],
                       preferred_element_type=f32)
        later = both[:2 * Q_BLOCK] + both[2 * Q_BLOCK:]
        return log_beta + later, jnp.sum(log_1m, axis=1, keepdims=True), causal

    def weighted_values(c, start, log_w, causal, run):
        a = jnp.exp(log_w if run is None else log_w + run)
        if causal is not None:
            a = jnp.where(causal, a, 0.0)
        vb = v_ref[0, pl.ds(start, a.shape[1]), c * LANES:(c + 1) * LANES]
        return jnp.dot(a.astype(bf16), vb, preferred_element_type=f32)

    def stacked_q(c):
        q = q_ref[0, :, c * LANES:(c + 1) * LANES]
        zq = jnp.zeros_like(q)
        return jnp.concatenate([jnp.where(head0, q, zq), jnp.where(head0, zq, q)], axis=0)

    diag_start = pl.multiple_of(i * Q_BLOCK, Q_BLOCK)
    n_wide = K_WIDE // Q_BLOCK

    def visit(windows, state):
        q2 = [stacked_q(c) for c in range(n_cols)]
        zs = [[scores(c, q2[c], start, width) for start, width, _ in windows] for c in range(n_cols)]
        terms = [[log_terms(z, diag) for z, (_, _, diag) in zip(zs[c], windows)] for c in range(n_cols)]
        for c in range(n_cols):
            run, acc = state[c] if state[c] is not None else (None, None)
            for (start, _, _), (log_w, row_sum, causal) in zip(windows, terms[c]):
                pv = weighted_values(c, start, log_w, causal, run)
                acc = pv if acc is None else acc + pv
                run = row_sum if run is None else run + row_sum
            run_ref[c] = run
            acc_ref[c] = acc

    fresh = [None] * n_cols
    wide_start = pl.multiple_of(diag_start - K_WIDE, Q_BLOCK)
    pl.when(i >= n_wide)(lambda: visit([(diag_start, Q_BLOCK, True), (wide_start, K_WIDE, False)], fresh))
    pl.when(i < n_wide)(lambda: visit([(diag_start, Q_BLOCK, True)], fresh))

    def unfinished():
        return jnp.max(run_ref[...]) > LOG_WEIGHT_ZERO

    def cond(state):
        jb, go = state
        return jnp.logical_and(jb >= 0, go)

    def body(state):
        jb, _ = state
        visit([(pl.multiple_of(jb * Q_BLOCK, Q_BLOCK), Q_BLOCK, False)],
              [(run_ref[c], acc_ref[c]) for c in range(n_cols)])
        return jb - 1, unfinished()

    first = jnp.where(i >= n_wide, i - 1 - n_wide, i - 1)
    lax.while_loop(cond, body, (first, unfinished()))

    for c in range(n_cols):
        cols = slice(c * LANES, (c + 1) * LANES)
        o = jnp.where(head0, acc_ref[c, 0:Q_BLOCK, :], acc_ref[c, Q_BLOCK:, :])
        o2 = o * o
        ssq0 = jnp.sum(jnp.where(head0, o2, 0.0), axis=-1, keepdims=True)
        ssq1 = jnp.sum(jnp.where(head0, 0.0, o2), axis=-1, keepdims=True)
        ms = jnp.where(head0, ssq0, ssq1) * (1.0 / HEAD_DIM)
        o_ref[0, :, cols] = (o * lax.rsqrt(ms + EPS) * g_ref[:, cols]).astype(o_ref.dtype)


def _attention(q, k, v, out_g, cols_per_step=4):
    bsz, s, d_attn = q.shape
    width = cols_per_step * LANES
    idx = lax.broadcasted_iota(jnp.int32, (K_WIDE, K_WIDE), 0)
    after = (idx > idx.T).astype(bf16)
    qspec = pl.BlockSpec((1, Q_BLOCK, width), lambda b, p, i: (b, i, p))
    kvspec = pl.BlockSpec((1, s, width), lambda b, p, i: (b, 0, p), pipeline_mode=pl.Buffered(1))
    return pl.pallas_call(
        _attn_kernel,
        grid=(bsz, d_attn // width, s // Q_BLOCK),
        in_specs=[qspec, kvspec, kvspec, pl.BlockSpec((1, width), lambda b, p, i: (0, p)),
                  _const_spec(after.shape)],
        out_specs=qspec,
        out_shape=jax.ShapeDtypeStruct((bsz, s, d_attn), bf16),
        scratch_shapes=[pltpu.VMEM((cols_per_step, 2 * Q_BLOCK, 1), f32),
                        pltpu.VMEM((cols_per_step, 2 * Q_BLOCK, LANES), f32)],
        name="stickbreak_attn",
        compiler_params=pltpu.CompilerParams(
            dimension_semantics=("arbitrary", "arbitrary", "arbitrary"), vmem_limit_bytes=VMEM_LIMIT),
    )(q, k, v, out_g, after)


def _ffn_kernel(x_ref, yp_ref, o_ref, mod_ref, g2_ref, w_out_ref, w_up_ref, cw_ref, cb_ref, w_down_ref,
                out_ref, halo_ref, act_ref, *, d_pool, d_ff):
    j = pl.program_id(1)
    tm = x_ref.shape[1]
    gate1 = mod_ref[0, 2:3, :]
    shift = mod_ref[0, 3:4, :]
    scale = mod_ref[0, 4:5, :]
    gate2 = mod_ref[0, 5:6, :]

    mixed = (jnp.dot(yp_ref[0], w_out_ref[0:d_pool, :], preferred_element_type=f32)
             + jnp.dot(o_ref[0], w_out_ref[d_pool:, :], preferred_element_type=f32))
    x1 = x_ref[0] + gate1 * mixed
    inv_rms = lax.rsqrt(jnp.mean(x1 * x1, axis=-1, keepdims=True) + EPS)
    hb = ((x1 * inv_rms) * (g2_ref[...] * (1.0 + scale)) + shift).astype(bf16)

    @pl.when(j == 0)
    def _():
        halo_ref[...] = jnp.zeros_like(halo_ref)

    def conv(c, cols):
        up = jnp.dot(hb, w_up_ref[:, cols], preferred_element_type=f32)
        ext = jnp.concatenate([halo_ref[c], up], axis=0)
        halo_ref[c] = up[tm - CONV_HALO:, :]
        prev1 = pltpu.roll(ext, 1, axis=0)[CONV_HALO:, :]
        prev2 = pltpu.roll(ext, 2, axis=0)[CONV_HALO:, :]
        return (cw_ref[2:3, cols] * up + cw_ref[1:2, cols] * prev1
                + cw_ref[0:1, cols] * prev2 + cb_ref[:, cols])

    n_chunks = d_ff // FF_CHUNK
    for c in range(n_chunks):
        gate = conv(c, slice(c * FF_CHUNK, (c + 1) * FF_CHUNK))
        val = conv(n_chunks + c, slice(d_ff + c * FF_CHUNK, d_ff + (c + 1) * FF_CHUNK))
        act = gate / (1.0 + jnp.exp(-gate)) * val
        act_ref[:, c * FF_CHUNK:(c + 1) * FF_CHUNK] = act.astype(bf16)

    ffn = jnp.dot(act_ref[...], w_down_ref[...], preferred_element_type=f32)
    out_ref[0] = x1 + gate2 * ffn


def _ffn(x, yp, o, mod, g2, w_out, w_up, conv_w, conv_b, w_down, tm=512):
    bsz, s, d = x.shape
    d_pool = yp.shape[-1]
    d_ff = w_down.shape[0]
    tile = lambda width: pl.BlockSpec((1, tm, width), lambda b, j: (b, j, 0))
    return pl.pallas_call(
        functools.partial(_ffn_kernel, d_pool=d_pool, d_ff=d_ff),
        grid=(bsz, s // tm),
        in_specs=[
            tile(d), tile(d_pool), tile(o.shape[-1]),
            pl.BlockSpec((1,) + mod.shape[1:], lambda b, j: (b, 0, 0)),
            _const_spec(g2.shape),
            _const_spec(w_out.shape),
            _const_spec(w_up.shape),
            _const_spec(conv_w.shape),
            _const_spec(conv_b.shape),
            _const_spec(w_down.shape),
        ],
        out_specs=tile(d),
        out_shape=jax.ShapeDtypeStruct((bsz, s, d), x.dtype),
        scratch_shapes=[
            pltpu.VMEM((2 * d_ff // FF_CHUNK, CONV_HALO, FF_CHUNK), f32),
            pltpu.VMEM((tm, d_ff), bf16),
        ],
        name="outproj_convffn",
        compiler_params=pltpu.CompilerParams(
            dimension_semantics=("arbitrary", "arbitrary"), vmem_limit_bytes=VMEM_LIMIT),
    )(x, yp, o, mod, g2, w_out, w_up, conv_w, conv_b, w_down)


def kernel(x, c, ada_w, ada_b, norm1_g, w_in, pool_w, pool_b, pool_scale, q_norm_g, k_norm_g, attn_out_g,
           w_out, norm2_g, w_up, conv_w, conv_b, w_down):
    depth = ada_w.shape[0]
    d = x.shape[-1]
    n_heads = attn_out_g.shape[1]
    row = lambda a: a.reshape(1, -1)
    for l in range(depth):
        mod = _adaln(c, ada_w[l], ada_b[l]).reshape(c.shape[0], -1, d)
        yp, q, k, v = _inproj(
            x, mod, row(norm1_g[l]), w_in[l].astype(bf16), pool_w[l].astype(bf16), row(pool_b[l]),
            row(pool_scale[l]), row(jnp.tile(q_norm_g[l], n_heads)), row(jnp.tile(k_norm_g[l], n_heads)))
        o = _attention(q, k, v, row(attn_out_g[l]))
        x = _ffn(x, yp, o, mod, row(norm2_g[l]), w_out[l].astype(bf16), w_up[l].astype(bf16),
                 conv_w[l], row(conv_b[l]), w_down[l].astype(bf16))
    return x
```

```python
import functools
import math

import jax
import jax.numpy as jnp
from jax import lax
from jax.experimental import pallas as pl
from jax.experimental.pallas import tpu as pltpu

EPS = 1e-6
HEAD_DIM = 64
N_POOL_GROUPS = 4
POOL_WINDOWS = (2, 4, 8, 16)
POOL_HALO = 16
CONV_WIDTH = 3
CONV_HALO = 8
LANES = 128
Q_BLOCK = 128
K_WIDE = 256
FF_CHUNK = 256
DOWN_GROUP = 11
LOG_WEIGHT_ZERO = -104.7
LOG2_E = math.log2(math.e)
VMEM_LIMIT = 48 * 1024 * 1024

bf16 = jnp.bfloat16
f32 = jnp.float32


def _const_spec(shape):
    nd = len(shape)
    return pl.BlockSpec(shape, lambda *_: (0,) * nd, pipeline_mode=pl.Buffered(1))


def _adaln_kernel(ct_ref, w_ref, b_ref, out_ref):
    ct = ct_ref[...]
    act = ct / (1.0 + jnp.exp(-ct))
    w = w_ref[...]
    rows = []
    for b in range(ct.shape[1]):
        rows.append(jnp.sum(act[:, b:b + 1] * w, axis=0, keepdims=True))
    out_ref[...] = jnp.concatenate(rows, axis=0) + b_ref[...]


def _adaln(c, ada_w, ada_b, tn=1536):
    bsz, d = c.shape
    n = ada_w.shape[1]
    return pl.pallas_call(
        _adaln_kernel,
        grid=(n // tn,),
        in_specs=[
            pl.BlockSpec((d, bsz), lambda j: (0, 0)),
            pl.BlockSpec((d, tn), lambda j: (0, j)),
            pl.BlockSpec((1, tn), lambda j: (0, j)),
        ],
        out_specs=pl.BlockSpec((bsz, tn), lambda j: (0, j)),
        out_shape=jax.ShapeDtypeStruct((bsz, n), f32),
        name="adaln",
        compiler_params=pltpu.CompilerParams(dimension_semantics=("arbitrary",)),
    )(c.T, ada_w, ada_b.reshape(1, n))


def _inproj_kernel(x_ref, mod_ref, g1_ref, w_in_ref, pool_w_ref, pool_b_ref, pool_s_ref,
                   qg_ref, kg_ref, headsum_ref,
                   yp_ref, q_ref, k_ref, v_ref, halo_ref, *, d_pool, d_attn):
    j = pl.program_id(1)
    tm = x_ref.shape[1]

    @pl.when(j == 0)
    def _():
        halo_ref[...] = jnp.zeros_like(halo_ref)

    x = x_ref[0]
    shift = mod_ref[0, 0:1, :]
    scale = mod_ref[0, 1:2, :]
    inv_rms = lax.rsqrt(jnp.mean(x * x, axis=-1, keepdims=True) + EPS)
    h = (x * inv_rms) * (g1_ref[...] * (1.0 + scale)) + shift
    hb = h.astype(bf16)

    project = lambda lo, hi: jnp.dot(hb, w_in_ref[:, lo:hi], preferred_element_type=f32)
    u = project(0, d_pool)
    tq = project(d_pool, d_pool + d_attn)
    tk = project(d_pool + d_attn, d_pool + 2 * d_attn)
    v_ref[0] = project(d_pool + 2 * d_attn, d_pool + 3 * d_attn).astype(v_ref.dtype)

    ext =jnp.concatenate([halo_ref[...], u], axis=0)
    halo_ref[...] = u[tm - POOL_HALO:, :]
    t1 = (j * tm + 1 + lax.broadcasted_iota(jnp.int32, (tm, 1), 0))
    ys = []
    for g, w in enumerate(POOL_WINDOWS):
        cols = slice(g * LANES, (g + 1) * LANES)
        s = ext[:, cols]
        span = 1
        while span < w:
            s = s + pltpu.roll(s, span, axis=0)
            span *= 2
        count = jnp.minimum(t1, w).astype(f32)
        pooled = s[POOL_HALO:, :] / count - u[:, cols]
        y = jnp.dot(pooled.astype(bf16), pool_w_ref[g], preferred_element_type=f32)
        ys.append((y + pool_b_ref[:, cols]) * pool_s_ref[:, cols])
    yp_ref[0] = jnp.concatenate(ys, axis=-1).astype(yp_ref.dtype)

    def head_norm(t, gain, post):
        ssq = jnp.dot((t * t).astype(bf16), headsum_ref[...], preferred_element_type=f32)
        return (t * lax.rsqrt(ssq * (1.0 / HEAD_DIM) + EPS)) * (gain * post)

    q_ref[0] = head_norm(tq, qg_ref[...], -1.0 / math.sqrt(HEAD_DIM)).astype(q_ref.dtype)
    k_ref[0] = head_norm(tk, kg_ref[...], 1.0).astype(k_ref.dtype)


def _inproj(x, mod, g1, w_in, pool_w, pool_b, pool_s, qg, kg, tm=512):
    bsz, s, d = x.shape
    d_pool = pool_b.shape[-1]
    d_attn = (w_in.shape[1] - d_pool) // 3
    n_heads = d_attn // HEAD_DIM
    headsum = jnp.kron(jnp.eye(n_heads, dtype=f32), jnp.ones((HEAD_DIM, HEAD_DIM), f32)).astype(bf16)
    tile = lambda width: pl.BlockSpec((1, tm, width), lambda b, j: (b, j, 0))
    out = jax.ShapeDtypeStruct((bsz, s, d_attn), bf16)
    return pl.pallas_call(
        functools.partial(_inproj_kernel, d_pool=d_pool, d_attn=d_attn),
        grid=(bsz, s // tm),
        in_specs=[
            tile(d),
            pl.BlockSpec((1,) + mod.shape[1:], lambda b, j: (b, 0, 0)),
            _const_spec(g1.shape),
            _const_spec(w_in.shape),
            _const_spec(pool_w.shape),
            _const_spec(pool_b.shape),
            _const_spec(pool_s.shape),
            _const_spec(qg.shape),
            _const_spec(kg.shape),
            _const_spec(headsum.shape),
        ],
        out_specs=[tile(d_pool), tile(d_attn), tile(d_attn), tile(d_attn)],
        out_shape=[jax.ShapeDtypeStruct((bsz, s, d_pool), bf16), out, out, out],
        scratch_shapes=[pltpu.VMEM((POOL_HALO, d_pool), f32)],
        name="inproj",
        compiler_params=pltpu.CompilerParams(
            dimension_semantics=("arbitrary", "arbitrary"), vmem_limit_bytes=VMEM_LIMIT),
    )(x, mod, g1, w_in, pool_w, pool_b, pool_s, qg, kg, headsum)


def _attn_kernel(q_ref, k_ref, v_ref, g_ref, from_ref, o_ref, run_ref, acc_ref):
    i = pl.program_id(2)
    n_cols = q_ref.shape[2] // LANES
    lane = lax.broadcasted_iota(jnp.int32, (1, LANES), 1)
    head0 = lane < HEAD_DIM

    def scores(c, q2, start, width):
        kb = k_ref[0, pl.ds(start, width), c * LANES:(c + 1) * LANES]
        return lax.dot_general(q2, kb, (((1,), (1,)), ((), ())), preferred_element_type=f32)

    def log_terms(zn, diag):
        width = zn.shape[1]
        log_1m = jnp.minimum(zn, 0.0) - jnp.log(1.0 + jnp.exp2(jnp.abs(zn) * -LOG2_E))
        causal = None
        if diag:
            row = lax.broadcasted_iota(jnp.int32, zn.shape, 0) & (Q_BLOCK - 1)
            causal = lax.broadcasted_iota(jnp.int32, zn.shape, 1) < row
            log_1m = jnp.where(causal, log_1m, 0.0)
        from_here = jnp.dot(log_1m.astype(bf16), from_ref[0:width, 0:width], preferred_element_type=f32)
        return from_here - zn, jnp.sum(log_1m, axis=1, keepdims=True), causal

    def weighted_values(c, start, log_w, causal, run):
        a = jnp.exp(log_w if run is None else log_w + run)
        if causal is not None:
            a = jnp.where(causal, a, 0.0)
        vb = v_ref[0, pl.ds(start, a.shape[1]), c * LANES:(c + 1) * LANES]
        return jnp.dot(a.astype(bf16), vb, preferred_element_type=f32)

    def stacked_q(c):
        q = q_ref[0, :, c * LANES:(c + 1) * LANES]
        zq = jnp.zeros_like(q)
        return jnp.concatenate([jnp.where(head0, q, zq), jnp.where(head0, zq, q)], axis=0)

    diag_start = pl.multiple_of(i * Q_BLOCK, Q_BLOCK)
    n_wide = K_WIDE // Q_BLOCK

    def visit(windows, state):
        q2 = [stacked_q(c) for c in range(n_cols)]
        zs = [[scores(c, q2[c], start, width) for start, width, _ in windows] for c in range(n_cols)]
        terms = [[log_terms(z, diag) for z, (_, _, diag) in zip(zs[c], windows)] for c in range(n_cols)]
        for c in range(n_cols):
            run, acc = state[c] if state[c] is not None else (None, None)
            for (start, _, _), (log_w, row_sum, causal) in zip(windows, terms[c]):
                pv = weighted_values(c, start, log_w, causal, run)
                acc = pv if acc is None else acc + pv
                run = row_sum if run is None else run + row_sum
            run_ref[c] = run
            acc_ref[c] = acc

    fresh = [None] * n_cols
    wide_start = pl.multiple_of(diag_start - K_WIDE, Q_BLOCK)
    pl.when(i >= n_wide)(lambda: visit([(diag_start, Q_BLOCK, True), (wide_start, K_WIDE, False)], fresh))
    pl.when(i < n_wide)(lambda: visit([(diag_start, Q_BLOCK, True)], fresh))

    def unfinished():
        return jnp.max(run_ref[...]) > LOG_WEIGHT_ZERO

    def cond(state):
        jb, go = state
        return jnp.logical_and(jb >= 0, go)

    def body(state):
        jb, _ = state
        visit([(pl.multiple_of(jb * Q_BLOCK, Q_BLOCK), Q_BLOCK, False)],
              [(run_ref[c], acc_ref[c]) for c in range(n_cols)])
        return jb - 1, unfinished()

    first = jnp.where(i >= n_wide, i - 1 - n_wide, i - 1)
    lax.while_loop(cond, body, (first, unfinished()))

    for c in range(n_cols):
        cols = slice(c * LANES, (c + 1) * LANES)
        o = jnp.where(head0, acc_ref[c, 0:Q_BLOCK, :], acc_ref[c, Q_BLOCK:, :])
        o2 = o * o
        ssq0 = jnp.sum(jnp.where(head0, o2, 0.0), axis=-1, keepdims=True)
        ssq1 = jnp.sum(jnp.where(head0, 0.0, o2), axis=-1, keepdims=True)
        ms = jnp.where(head0, ssq0, ssq1) * (1.0 / HEAD_DIM)
        o_ref[0, :, cols] = (o * lax.rsqrt(ms + EPS) * g_ref[:, cols]).astype(o_ref.dtype)


def _attention(q, k, v, out_g, cols_per_step=4):
    bsz, s, d_attn = q.shape
    width = cols_per_step * LANES
    idx = lax.broadcasted_iota(jnp.int32, (K_WIDE, K_WIDE), 0)
    from_here = (idx >= idx.T).astype(bf16)
    qspec = pl.BlockSpec((1, Q_BLOCK, width), lambda b, p, i: (b, i, p))
    kvspec = pl.BlockSpec((1, s, width), lambda b, p, i: (b, 0, p), pipeline_mode=pl.Buffered(1))
    return pl.pallas_call(
        _attn_kernel,
        grid=(bsz, d_attn // width, s // Q_BLOCK),
        in_specs=[qspec, kvspec, kvspec, pl.BlockSpec((1, width), lambda b, p, i: (0, p)),
                  _const_spec(from_here.shape)],
        out_specs=qspec,
        out_shape=jax.ShapeDtypeStruct((bsz, s, d_attn), bf16),
        scratch_shapes=[pltpu.VMEM((cols_per_step, 2 * Q_BLOCK, 1), f32),
                        pltpu.VMEM((cols_per_step, 2 * Q_BLOCK, LANES), f32)],
        name="stickbreak_attn",
        compiler_params=pltpu.CompilerParams(
            dimension_semantics=("arbitrary", "arbitrary", "arbitrary"), vmem_limit_bytes=VMEM_LIMIT),
    )(q, k, v, out_g, from_here)


def _ffn_kernel(x_ref, yp_ref, o_ref, mod_ref, g2_ref, w_out_ref, w_up_ref, cw_ref, cb_ref, w_down_ref,
                out_ref, halo_ref, act_ref, *, d_pool, d_ff):
    j = pl.program_id(1)
    tm = x_ref.shape[1]
    gate1 = mod_ref[0, 2:3, :]
    shift = mod_ref[0, 3:4, :]
    scale = mod_ref[0, 4:5, :]
    gate2 = mod_ref[0, 5:6, :]

    @pl.when(j == 0)
    def _():
        halo_ref[...] = jnp.zeros_like(halo_ref)

    mixed =(jnp.dot(yp_ref[0], w_out_ref[0:d_pool, :], preferred_element_type=f32)
             + jnp.dot(o_ref[0], w_out_ref[d_pool:, :], preferred_element_type=f32))
    x1 = x_ref[0] + gate1 * mixed
    inv_rms = lax.rsqrt(jnp.mean(x1 * x1, axis=-1, keepdims=True) + EPS)
    hb = ((x1 * inv_rms) * (g2_ref[...] * (1.0 + scale)) + shift).astype(bf16)

    def conv(c, cols):
        up = jnp.dot(hb, w_up_ref[:, cols], preferred_element_type=f32)
        ext = jnp.concatenate([halo_ref[c], up], axis=0)
        halo_ref[c] = up[tm - CONV_HALO:, :]
        prev1 = pltpu.roll(ext, 1, axis=0)[CONV_HALO:, :]
        prev2 = pltpu.roll(ext, 2, axis=0)[CONV_HALO:, :]
        return (cw_ref[2:3, cols] * up + cw_ref[1:2, cols] * prev1
                + cw_ref[0:1, cols] * prev2 + cb_ref[:, cols])

    n_chunks = d_ff // FF_CHUNK

    def up_chunk(c):
        gate = conv(c, slice(c * FF_CHUNK, (c + 1) * FF_CHUNK))
        val = conv(n_chunks + c, slice(d_ff + c * FF_CHUNK, d_ff + (c + 1) * FF_CHUNK))
        act = gate / (1.0 + jnp.exp(-gate)) * val
        act_ref[:, c * FF_CHUNK:(c + 1) * FF_CHUNK] = act.astype(bf16)

    ffn = None
    done = 0
    for c in range(n_chunks):
        up_chunk(c)
        ready = c if c < n_chunks - 1 else n_chunks
        if ready - done >= DOWN_GROUP or ready == n_chunks:
            rows = slice(done * FF_CHUNK, ready * FF_CHUNK)
            part = jnp.dot(act_ref[:, rows], w_down_ref[rows, :], preferred_element_type=f32)
            ffn = part if ffn is None else ffn + part
            done = ready
    out_ref[0] = x1 + gate2 * ffn


def _ffn(x, yp, o, mod, g2, w_out, w_up, conv_w, conv_b, w_down, tm=512):
    bsz, s, d = x.shape
    d_pool = yp.shape[-1]
    d_ff = w_down.shape[0]
    tile = lambda width: pl.BlockSpec((1, tm, width), lambda b, j: (b, j, 0))
    return pl.pallas_call(
        functools.partial(_ffn_kernel, d_pool=d_pool, d_ff=d_ff),
        grid=(bsz, s // tm),
        in_specs=[
            tile(d), tile(d_pool), tile(o.shape[-1]),
            pl.BlockSpec((1,) + mod.shape[1:], lambda b, j: (b, 0, 0)),
            _const_spec(g2.shape),
            _const_spec(w_out.shape),
            _const_spec(w_up.shape),
            _const_spec(conv_w.shape),
            _const_spec(conv_b.shape),
            _const_spec(w_down.shape),
        ],
        out_specs=tile(d),
        out_shape=jax.ShapeDtypeStruct((bsz, s, d), x.dtype),
        scratch_shapes=[
            pltpu.VMEM((2 * d_ff // FF_CHUNK, CONV_HALO, FF_CHUNK), f32),
            pltpu.VMEM((tm, d_ff), bf16),
        ],
        name="outproj_convffn",
        compiler_params=pltpu.CompilerParams(
            dimension_semantics=("arbitrary", "arbitrary"), vmem_limit_bytes=VMEM_LIMIT),
    )(x, yp, o, mod, g2, w_out, w_up, conv_w, conv_b, w_down)


def kernel(x, c, ada_w, ada_b, norm1_g, w_in, pool_w, pool_b, pool_scale, q_norm_g, k_norm_g, attn_out_g,
           w_out, norm2_g, w_up, conv_w, conv_b, w_down):
    depth = ada_w.shape[0]
    d = x.shape[-1]
    n_heads = attn_out_g.shape[1]
    row = lambda a: a.reshape(1, -1)
    for l in range(depth):
        mod = _adaln(c, ada_w[l], ada_b[l]).reshape(c.shape[0], -1, d)
        yp, q, k, v = _inproj(
            x, mod, row(norm1_g[l]), w_in[l].astype(bf16), pool_w[l].astype(bf16), row(pool_b[l]),
            row(pool_scale[l]), row(jnp.tile(q_norm_g[l], n_heads)), row(jnp.tile(k_norm_g[l], n_heads)))
        o = _attention(q, k, v, row(attn_out_g[l]))
        x = _ffn(x, yp, o, mod, row(norm2_g[l]), w_out[l].astype(bf16), w_up[l].astype(bf16),
                 conv_w[l], row(conv_b[l]), w_down[l].astype(bf16))
    return x
```

```python
import collections
import functools
import itertools
import math

import jax
import jax.numpy as jnp
from jax import lax
from jax.experimental import pallas as pl
from jax.experimental.pallas import tpu as pltpu

EPS = 1e-6
HEAD_DIM = 64
N_POOL_GROUPS = 4
POOL_WINDOWS = (2, 4, 8, 16)
POOL_HALO = 16
CONV_WIDTH = 3
CONV_HALO = 8
LANES = 128
Q_BLOCK = 128
K_WIDE = 256
FF_CHUNK = 256
ATTN_SKEW = 1
LOG_WEIGHT_ZERO = -104.7
LOG2_E = math.log2(math.e)
VMEM_LIMIT = 48 * 1024 * 1024
FUSED_VMEM_LIMIT = 58 * 1024 * 1024

bf16 = jnp.bfloat16
f32 = jnp.float32

_Window = collections.namedtuple("_Window", "k v start width diag prev")


def _const_spec(shape):
    nd = len(shape)
    return pl.BlockSpec(shape, lambda *_: (0,) * nd, pipeline_mode=pl.Buffered(1))


def _adaln_kernel(ct_ref, w_ref, b_ref, out_ref):
    ct = ct_ref[...]
    act = ct / (1.0 + jnp.exp(-ct))
    w = w_ref[...]
    rows = []
    for b in range(ct.shape[1]):
        rows.append(jnp.sum(act[:, b:b + 1] * w, axis=0, keepdims=True))
    out_ref[...] = jnp.concatenate(rows, axis=0) + b_ref[...]


def _adaln(c, ada_w, ada_b, tn=1536):
    bsz, d = c.shape
    n = ada_w.shape[1]
    return pl.pallas_call(
        _adaln_kernel,
        grid=(n // tn,),
        in_specs=[
            pl.BlockSpec((d, bsz), lambda j: (0, 0)),
            pl.BlockSpec((d, tn), lambda j: (0, j)),
            pl.BlockSpec((1, tn), lambda j: (0, j)),
        ],
        out_specs=pl.BlockSpec((bsz, tn), lambda j: (0, j)),
        out_shape=jax.ShapeDtypeStruct((bsz, n), f32),
        name="adaln",
        compiler_params=pltpu.CompilerParams(dimension_semantics=("arbitrary",)),
    )(c.T, ada_w, ada_b.reshape(1, n))


def _inproj_kernel(x_ref, mod_ref, g1_ref, w_in_ref, pool_w_ref, pool_b_ref, pool_s_ref,
                   qg_ref, kg_ref, headsum_ref,
                   yp_ref, q_ref, k_ref, v_ref, halo_ref, *, d_pool, d_attn):
    j = pl.program_id(1)
    tm = x_ref.shape[1]

    @pl.when(j == 0)
    def _():
        halo_ref[...] = jnp.zeros_like(halo_ref)

    x = x_ref[0]
    shift = mod_ref[0, 0:1, :]
    scale = mod_ref[0, 1:2, :]
    inv_rms = lax.rsqrt(jnp.mean(x * x, axis=-1, keepdims=True) + EPS)
    h = (x * inv_rms) * (g1_ref[...] * (1.0 + scale)) + shift
    hb = h.astype(bf16)

    project = lambda lo, hi: jnp.dot(hb, w_in_ref[:, lo:hi], preferred_element_type=f32)
    u = project(0, d_pool)
    tq = project(d_pool, d_pool + d_attn)
    tk = project(d_pool + d_attn, d_pool + 2 * d_attn)
    v_ref[0] = project(d_pool + 2 * d_attn, d_pool + 3 * d_attn).astype(v_ref.dtype)

    ext = jnp.concatenate([halo_ref[...], u], axis=0)
    halo_ref[...] = u[tm - POOL_HALO:, :]
    t1 = (j * tm + 1 + lax.broadcasted_iota(jnp.int32, (tm, 1), 0))
    ys = []
    for g, w in enumerate(POOL_WINDOWS):
        cols = slice(g * LANES, (g + 1) * LANES)
        s = ext[:, cols]
        span = 1
        while span < w:
            s = s + pltpu.roll(s, span, axis=0)
            span *= 2
        count = jnp.minimum(t1, w).astype(f32)
        pooled = s[POOL_HALO:, :] / count - u[:, cols]
        y = jnp.dot(pooled.astype(bf16), pool_w_ref[g], preferred_element_type=f32)
        ys.append((y + pool_b_ref[:, cols]) * pool_s_ref[:, cols])
    yp_ref[0] = jnp.concatenate(ys, axis=-1).astype(yp_ref.dtype)

    def head_norm(t, gain, post):
        ssq = jnp.dot((t * t).astype(bf16), headsum_ref[...], preferred_element_type=f32)
        return (t * lax.rsqrt(ssq * (1.0 / HEAD_DIM) + EPS)) * (gain * post)

    q_ref[0] = head_norm(tq, qg_ref[...], -1.0 / math.sqrt(HEAD_DIM)).astype(q_ref.dtype)
    k_ref[0] = head_norm(tk, kg_ref[...], 1.0).astype(k_ref.dtype)


def _inproj(x, mod, g1, w_in, pool_w, pool_b, pool_s, qg, kg, tm=512):
    bsz, s, d = x.shape
    d_pool = pool_b.shape[-1]
    d_attn = (w_in.shape[1] - d_pool) // 3
    n_heads = d_attn // HEAD_DIM
    headsum = jnp.kron(jnp.eye(n_heads, dtype=f32), jnp.ones((HEAD_DIM, HEAD_DIM), f32)).astype(bf16)
    tile = lambda width: pl.BlockSpec((1, tm, width), lambda b, j: (b, j, 0))
    out = jax.ShapeDtypeStruct((bsz, s, d_attn), bf16)
    return pl.pallas_call(
        functools.partial(_inproj_kernel, d_pool=d_pool, d_attn=d_attn),
        grid=(bsz, s // tm),
        in_specs=[
            tile(d),
            pl.BlockSpec((1,) + mod.shape[1:], lambda b, j: (b, 0, 0)),
            _const_spec(g1.shape),
            _const_spec(w_in.shape),
            _const_spec(pool_w.shape),
            _const_spec(pool_b.shape),
            _const_spec(pool_s.shape),
            _const_spec(qg.shape),
            _const_spec(kg.shape),
            _const_spec(headsum.shape),
        ],
        out_specs=[tile(d_pool), tile(d_attn), tile(d_attn), tile(d_attn)],
        out_shape=[jax.ShapeDtypeStruct((bsz, s, d_pool), bf16), out, out, out],
        scratch_shapes=[pltpu.VMEM((POOL_HALO, d_pool), f32)],
        name="inproj",
        compiler_params=pltpu.CompilerParams(
            dimension_semantics=("arbitrary", "arbitrary"), vmem_limit_bytes=VMEM_LIMIT),
    )(x, mod, g1, w_in, pool_w, pool_b, pool_s, qg, kg, headsum)


def _attn_ffn_kernel(q_ref, kc_ref, vc_ref, kp_ref, vp_ref, k_hbm, v_hbm, og_ref, from_ref,
                     x_ref, yp_ref, mod_ref, g2_ref, w_out_ref, w_up_ref, cw_ref, cb_ref, w_down_ref,
                     out_ref,
                     o_scr, halo_ref, act_ref, run_ref, acc_ref, kbuf, vbuf, sem,
                     *, tiles_per_batch, d_pool, d_ff):
    s = pl.program_id(0)
    n_tiles = pl.num_programs(0) - 1
    ta = jnp.minimum(s, n_tiles - 1)
    ta_in_batch = lax.rem(ta, tiles_per_batch)
    batch_a = lax.div(ta, tiles_per_batch)
    tf_in_batch = lax.rem(jnp.maximum(s - 1, 0), tiles_per_batch)
    tm = q_ref.shape[1]
    n_qblk = tm // Q_BLOCK
    n_cols = q_ref.shape[2] // LANES

    @pl.when(s == 0)
    def _():
        o_scr[...] = jnp.zeros_like(o_scr)

    @pl.when(tf_in_batch == 0)
    def _():
        halo_ref[...] = jnp.zeros_like(halo_ref)

    lane = lax.broadcasted_iota(jnp.int32, (1, LANES), 1)
    head0 = lane < HEAD_DIM
    prev_valid = ta_in_batch > 0

    def rows(ref, start, width, c):
        cols = slice(c * LANES, (c + 1) * LANES)
        if len(ref.shape) == 3:
            return ref[0, start:start + width, cols]
        return ref[start:start + width, cols]

    def stacked_q(r, c):
        q = q_ref[0, r * Q_BLOCK:(r + 1) * Q_BLOCK, c * LANES:(c + 1) * LANES]
        zq = jnp.zeros_like(q)
        return jnp.concatenate([jnp.where(head0, q, zq), jnp.where(head0, zq, q)], axis=0)

    def scores(c, q2, win):
        kb = rows(win.k, win.start, win.width, c)
        return lax.dot_general(q2, kb, (((1,), (1,)), ((), ())), preferred_element_type=f32)

    def log_terms(zn, win):
        width = zn.shape[1]
        log_1m = jnp.minimum(zn, 0.0) - jnp.log(1.0 + jnp.exp2(jnp.abs(zn) * -LOG2_E))
        mask = None
        if win.diag:
            row = lax.broadcasted_iota(jnp.int32, zn.shape, 0) & (Q_BLOCK - 1)
            mask = lax.broadcasted_iota(jnp.int32, zn.shape, 1) < row
        elif win.prev:
            mask = prev_valid
        if mask is not None:
            log_1m = jnp.where(mask, log_1m, 0.0)
        from_here = jnp.dot(log_1m.astype(bf16), from_ref[0:width, 0:width], preferred_element_type=f32)
        return from_here - zn, jnp.sum(log_1m, axis=1, keepdims=True), mask

    def weighted_values(c, win, log_w, mask, run):
        a = jnp.exp(log_w if run is None else log_w + run)
        if mask is not None:
            a = jnp.where(mask, a, 0.0)
        return jnp.dot(a.astype(bf16), rows(win.v, win.start, win.width, c), preferred_element_type=f32)

    def visit(r, windows, fresh):
        q2 = [stacked_q(r, c) for c in range(n_cols)]
        zs = [[scores(c, q2[c], w) for w in windows] for c in range(n_cols)]
        yield
        terms = [[log_terms(z, w) for z, w in zip(zs[c], windows)] for c in range(n_cols)]
        yield
        for c in range(n_cols):
            slot = r * n_cols + c
            run, acc = (None, None) if fresh else (run_ref[slot], acc_ref[slot])
            for w, (log_w, row_sum, mask) in zip(windows, terms[c]):
                pv = weighted_values(c, w, log_w, mask, run)
                acc = pv if acc is None else acc + pv
                run = row_sum if run is None else run + row_sum
            run_ref[slot] = run
            acc_ref[slot] = acc
        yield

    def first_windows(r):
        wins = [_Window(kc_ref, vc_ref, r * Q_BLOCK, Q_BLOCK, True, False)]
        start = r * Q_BLOCK - K_WIDE
        if start >= 0:
            wins.append(_Window(kc_ref, vc_ref, start, K_WIDE, False, False))
        else:
            if r > 0:
                wins.append(_Window(kc_ref, vc_ref, 0, r * Q_BLOCK, False, False))
            wins.append(_Window(kp_ref, vp_ref, K_WIDE + start, -start, False, True))
        return wins

    def attention_main():
        blocks = [visit(r, first_windows(r), True) for r in range(n_qblk)]
        n_stages = 3
        for step in range(n_qblk + (n_stages - 1) * ATTN_SKEW):
            for r in reversed(range(n_qblk)):
                stage, phase = divmod(step - r, ATTN_SKEW)
                if phase == 0 and 0 <= stage < n_stages and step - r >= 0:
                    next(blocks[r])
                    yield

    gate1 = mod_ref[0, 2:3, :]
    shift = mod_ref[0, 3:4, :]
    scale = mod_ref[0, 4:5, :]
    gate2 = mod_ref[0, 5:6, :]
    n_chunks = d_ff // FF_CHUNK

    def ffn_main():
        mixed = (jnp.dot(yp_ref[0], w_out_ref[0:d_pool, :], preferred_element_type=f32)
                 + jnp.dot(o_scr[...], w_out_ref[d_pool:, :], preferred_element_type=f32))
        x1 = x_ref[0] + gate1 * mixed
        inv_rms = lax.rsqrt(jnp.mean(x1 * x1, axis=-1, keepdims=True) + EPS)
        hb = ((x1 * inv_rms) * (g2_ref[...] * (1.0 + scale)) + shift).astype(bf16)
        yield

        def conv(c, cols):
            up = jnp.dot(hb, w_up_ref[:, cols], preferred_element_type=f32)
            ext = jnp.concatenate([halo_ref[c], up], axis=0)
            halo_ref[c] = up[tm - CONV_HALO:, :]
            prev1 = pltpu.roll(ext, 1, axis=0)[CONV_HALO:, :]
            prev2 = pltpu.roll(ext, 2, axis=0)[CONV_HALO:, :]
            return (cw_ref[2:3, cols] * up + cw_ref[1:2, cols] * prev1
                    + cw_ref[0:1, cols] * prev2 + cb_ref[:, cols])

        for c in range(n_chunks):
            gate = conv(c, slice(c * FF_CHUNK, (c + 1) * FF_CHUNK))
            val = conv(n_chunks + c, slice(d_ff + c * FF_CHUNK, d_ff + (c + 1) * FF_CHUNK))
            act = gate / (1.0 + jnp.exp(-gate)) * val
            act_ref[:, c * FF_CHUNK:(c + 1) * FF_CHUNK] = act.astype(bf16)
            yield
        ffn = jnp.dot(act_ref[...], w_down_ref[...], preferred_element_type=f32)
        out_ref[0] = x1 + gate2 * ffn
        yield

    for _ in itertools.zip_longest(attention_main(), ffn_main()):
        pass

    def unfinished():
        return jnp.max(run_ref[...]) > LOG_WEIGHT_ZERO

    def drain(gen):
        for _ in gen:
            pass

    @pl.when(unfinished())
    def _():
        for r in range(1, n_qblk):
            wins = []
            seen = r * Q_BLOCK - K_WIDE
            if seen > 0:
                wins.append(_Window(kc_ref, vc_ref, 0, seen, False, False))
            top = K_WIDE + min(seen, 0)
            wins.append(_Window(kp_ref, vp_ref, 0, top, False, True))
            drain(visit(r, wins, False))

    def cond(state):
        jb, go = state
        return jnp.logical_and(jb >= 0, go)

    def body(state):
        jb, _ = state
        src = pl.ds(pl.multiple_of(jb * Q_BLOCK, Q_BLOCK), Q_BLOCK)
        copies = [pltpu.make_async_copy(k_hbm.at[batch_a, src, :], kbuf, sem.at[0]),
                  pltpu.make_async_copy(v_hbm.at[batch_a, src, :], vbuf, sem.at[1])]
        for cp in copies:
            cp.start()
        for cp in copies:
            cp.wait()
        for r in range(n_qblk):
            drain(visit(r, [_Window(kbuf, vbuf, 0, Q_BLOCK, False, False)], False))
        return jb - 1, unfinished()

    first = ta_in_batch * n_qblk - (K_WIDE // Q_BLOCK + 1)
    lax.while_loop(cond, body, (first, unfinished()))

    for r in range(n_qblk):
        for c in range(n_cols):
            cols = slice(c * LANES, (c + 1) * LANES)
            acc = acc_ref[r * n_cols + c]
            o = jnp.where(head0, acc[0:Q_BLOCK, :], acc[Q_BLOCK:, :])
            o2 = o * o
            ssq0 = jnp.sum(jnp.where(head0, o2, 0.0), axis=-1, keepdims=True)
            ssq1 = jnp.sum(jnp.where(head0, 0.0, o2), axis=-1, keepdims=True)
            ms = jnp.where(head0, ssq0, ssq1) * (1.0 / HEAD_DIM)
            o_scr[r * Q_BLOCK:(r + 1) * Q_BLOCK, cols] = (o * lax.rsqrt(ms + EPS) * og_ref[:, cols]).astype(o_scr.dtype)


def _attn_ffn(x, yp, q, k, v, mod, out_g, g2, w_out, w_up, conv_w, conv_b, w_down, tm=512):
    bsz, s, d = x.shape
    d_pool = yp.shape[-1]
    d_attn = q.shape[-1]
    d_ff = w_down.shape[0]
    tpb = s // tm
    n_tiles = bsz * tpb
    n_slots = (tm // Q_BLOCK) * (d_attn // LANES)
    idx = lax.broadcasted_iota(jnp.int32, (K_WIDE, K_WIDE), 0)
    from_here = (idx >= idx.T).astype(bf16)

    def attn_tile(step):
        t = jnp.minimum(step, n_tiles - 1)
        return t // tpb, t % tpb

    def ffn_tile(step):
        t = jnp.maximum(step - 1, 0)
        return t // tpb, t % tpb

    def a_spec(width):
        return pl.BlockSpec((1, tm, width), lambda i: (*attn_tile(i), 0))

    def prev_spec(width):
        per_tile = tm // K_WIDE
        return pl.BlockSpec((1, K_WIDE, width),
                            lambda i: (attn_tile(i)[0], jnp.maximum(attn_tile(i)[1] * per_tile - 1, 0), 0))

    def f_spec(width):
        return pl.BlockSpec((1, tm, width), lambda i: (*ffn_tile(i), 0))

    hbm = pl.BlockSpec(memory_space=pl.ANY)
    return pl.pallas_call(
        functools.partial(_attn_ffn_kernel, tiles_per_batch=tpb, d_pool=d_pool, d_ff=d_ff),
        grid=(n_tiles + 1,),
        in_specs=[
            a_spec(d_attn), a_spec(d_attn), a_spec(d_attn), prev_spec(d_attn), prev_spec(d_attn), hbm, hbm,
            _const_spec(out_g.shape), _const_spec(from_here.shape),
            f_spec(d), f_spec(d_pool),
            pl.BlockSpec((1,) + mod.shape[1:], lambda i: (ffn_tile(i)[0], 0, 0)),
            _const_spec(g2.shape),
            _const_spec(w_out.shape),
            _const_spec(w_up.shape),
            _const_spec(conv_w.shape),
            _const_spec(conv_b.shape),
            _const_spec(w_down.shape),
        ],
        out_specs=f_spec(d),
        out_shape=jax.ShapeDtypeStruct((bsz, s, d), x.dtype),
        scratch_shapes=[
            pltpu.VMEM((tm, d_attn), bf16),
            pltpu.VMEM((2 * d_ff // FF_CHUNK, CONV_HALO, FF_CHUNK), f32),
            pltpu.VMEM((tm, d_ff), bf16),
            pltpu.VMEM((n_slots, 2 * Q_BLOCK, 1), f32),
            pltpu.VMEM((n_slots, 2 * Q_BLOCK, LANES), f32),
            pltpu.VMEM((Q_BLOCK, d_attn), bf16),
            pltpu.VMEM((Q_BLOCK, d_attn), bf16),
            pltpu.SemaphoreType.DMA((2,)),
        ],
        name="attn_convffn",
        compiler_params=pltpu.CompilerParams(
            dimension_semantics=("arbitrary",), vmem_limit_bytes=FUSED_VMEM_LIMIT),
    )(q, k, v, k, v, k, v, out_g, from_here, x, yp, mod, g2, w_out, w_up, conv_w, conv_b, w_down)


def kernel(x, c, ada_w, ada_b, norm1_g, w_in, pool_w, pool_b, pool_scale, q_norm_g, k_norm_g, attn_out_g,
           w_out, norm2_g, w_up, conv_w, conv_b, w_down):
    depth = ada_w.shape[0]
    d = x.shape[-1]
    n_heads = attn_out_g.shape[1]
    row = lambda a: a.reshape(1, -1)
    for l in range(depth):
        mod = _adaln(c, ada_w[l], ada_b[l]).reshape(c.shape[0], -1, d)
        yp, q, k, v = _inproj(
            x, mod, row(norm1_g[l]), w_in[l].astype(bf16), pool_w[l].astype(bf16), row(pool_b[l]),
            row(pool_scale[l]), row(jnp.tile(q_norm_g[l], n_heads)), row(jnp.tile(k_norm_g[l], n_heads)))
        x = _attn_ffn(x, yp, q, k, v, mod, row(attn_out_g[l]), row(norm2_g[l]), w_out[l].astype(bf16),
                      w_up[l].astype(bf16), conv_w[l], row(conv_b[l]), w_down[l].astype(bf16))
    return x
```

```python
import collections
import functools
import itertools
import math

import jax
import jax.numpy as jnp
from jax import lax
from jax.experimental import pallas as pl
from jax.experimental.pallas import tpu as pltpu

EPS = 1e-6
HEAD_DIM = 64
N_POOL_GROUPS = 4
POOL_WINDOWS = (2, 4, 8, 16)
POOL_HALO = 16
CONV_WIDTH = 3
CONV_HALO = 8
LANES = 128
Q_BLOCK = 128
K_WIDE = 256
FF_CHUNK = 256
ATTN_SKEW = 1
LOG_WEIGHT_ZERO = -104.7
LOG2_E = math.log2(math.e)
VMEM_LIMIT = 48 * 1024 * 1024
FUSED_VMEM_LIMIT = 58 * 1024 * 1024

bf16 = jnp.bfloat16
f32 = jnp.float32

_Window = collections.namedtuple("_Window", "k v start width diag prev")


def _const_spec(shape):
    nd = len(shape)
    return pl.BlockSpec(shape, lambda *_: (0,) * nd, pipeline_mode=pl.Buffered(1))


def _adaln_kernel(ct_ref, w_ref, b_ref, out_ref):
    ct = ct_ref[...]
    act = ct / (1.0 + jnp.exp(-ct))
    w = w_ref[...]
    rows = []
    for b in range(ct.shape[1]):
        rows.append(jnp.sum(act[:, b:b + 1] * w, axis=0, keepdims=True))
    out_ref[...] = jnp.concatenate(rows, axis=0) + b_ref[...]


def _adaln(c, ada_w, ada_b, tn=1536):
    bsz, d = c.shape
    n = ada_w.shape[1]
    return pl.pallas_call(
        _adaln_kernel,
        grid=(n // tn,),
        in_specs=[
            pl.BlockSpec((d, bsz), lambda j: (0, 0)),
            pl.BlockSpec((d, tn), lambda j: (0, j)),
            pl.BlockSpec((1, tn), lambda j: (0, j)),
        ],
        out_specs=pl.BlockSpec((bsz, tn), lambda j: (0, j)),
        out_shape=jax.ShapeDtypeStruct((bsz, n), f32),
        name="adaln",
        compiler_params=pltpu.CompilerParams(dimension_semantics=("arbitrary",)),
    )(c.T, ada_w, ada_b.reshape(1, n))


def _inproj_kernel(x_ref, mod_ref, g1_ref, w_in_ref, pool_w_ref, pool_b_ref, pool_s_ref,
                   qg_ref, kg_ref,
                   yp_ref, q_ref, k_ref, v_ref, halo_ref, *, d_pool, d_attn):
    j = pl.program_id(1)
    tm = x_ref.shape[1]

    @pl.when(j == 0)
    def _():
        halo_ref[...] = jnp.zeros_like(halo_ref)

    x = x_ref[0]
    shift = mod_ref[0, 0:1, :]
    scale = mod_ref[0, 1:2, :]
    inv_rms = lax.rsqrt(jnp.mean(x * x, axis=-1, keepdims=True) + EPS)
    h = (x * inv_rms) * (g1_ref[...] * (1.0 + scale)) + shift
    hb = h.astype(bf16)

    project = lambda lo, hi: jnp.dot(hb, w_in_ref[:, lo:hi], preferred_element_type=f32)
    u = project(0, d_pool)
    tq = project(d_pool, d_pool + d_attn)
    tk = project(d_pool + d_attn, d_pool + 2 * d_attn)
    v_ref[0] = project(d_pool + 2 * d_attn, d_pool + 3 * d_attn).astype(v_ref.dtype)

    ext = jnp.concatenate([halo_ref[...], u], axis=0)
    halo_ref[...] = u[tm - POOL_HALO:, :]
    t1 = (j * tm + 1 + lax.broadcasted_iota(jnp.int32, (tm, 1), 0))
    ys = []
    for g, w in enumerate(POOL_WINDOWS):
        cols = slice(g * LANES, (g + 1) * LANES)
        s = ext[:, cols]
        span = 1
        while span < w:
            s = s + pltpu.roll(s, span, axis=0)
            span *= 2
        count = jnp.minimum(t1, w).astype(f32)
        pooled = s[POOL_HALO:, :] / count - u[:, cols]
        y = jnp.dot(pooled.astype(bf16), pool_w_ref[g], preferred_element_type=f32)
        ys.append((y + pool_b_ref[:, cols]) * pool_s_ref[:, cols])
    yp_ref[0] = jnp.concatenate(ys, axis=-1).astype(yp_ref.dtype)

    head0 = lax.broadcasted_iota(jnp.int32, (1, LANES), 1) < HEAD_DIM

    def head_norm(t, gain, post):
        inv = []
        for c in range(d_attn // LANES):
            sq = t[:, c * LANES:(c + 1) * LANES]
            sq = sq * sq
            ssq0 = jnp.sum(jnp.where(head0, sq, 0.0), axis=-1, keepdims=True)
            ssq1 = jnp.sum(jnp.where(head0, 0.0, sq), axis=-1, keepdims=True)
            inv.append(jnp.where(head0, lax.rsqrt(ssq0 * (1.0 / HEAD_DIM) + EPS),
                                 lax.rsqrt(ssq1 * (1.0 / HEAD_DIM) + EPS)))
        return (t * jnp.concatenate(inv, axis=-1)) * (gain * post)

    q_ref[0] = head_norm(tq, qg_ref[...], -1.0 / math.sqrt(HEAD_DIM)).astype(q_ref.dtype)
    k_ref[0] = head_norm(tk, kg_ref[...], 1.0).astype(k_ref.dtype)


def _inproj(x, mod, g1, w_in, pool_w, pool_b, pool_s, qg, kg, tm=512):
    bsz, s, d = x.shape
    d_pool = pool_b.shape[-1]
    d_attn = (w_in.shape[1] - d_pool) // 3
    tile = lambda width: pl.BlockSpec((1, tm, width), lambda b, j: (b, j, 0))
    out = jax.ShapeDtypeStruct((bsz, s, d_attn), bf16)
    return pl.pallas_call(
        functools.partial(_inproj_kernel, d_pool=d_pool, d_attn=d_attn),
        grid=(bsz, s // tm),
        in_specs=[
            tile(d),
            pl.BlockSpec((1,) + mod.shape[1:], lambda b, j: (b, 0, 0)),
            _const_spec(g1.shape),
            _const_spec(w_in.shape),
            _const_spec(pool_w.shape),
            _const_spec(pool_b.shape),
            _const_spec(pool_s.shape),
            _const_spec(qg.shape),
            _const_spec(kg.shape),
        ],
        out_specs=[tile(d_pool), tile(d_attn), tile(d_attn), tile(d_attn)],
        out_shape=[jax.ShapeDtypeStruct((bsz, s, d_pool), bf16), out, out, out],
        scratch_shapes=[pltpu.VMEM((POOL_HALO, d_pool), f32)],
        name="inproj",
        compiler_params=pltpu.CompilerParams(
            dimension_semantics=("arbitrary", "arbitrary"), vmem_limit_bytes=VMEM_LIMIT),
    )(x, mod, g1, w_in, pool_w, pool_b, pool_s, qg, kg)


def _attn_ffn_kernel(q_ref, kc_ref, vc_ref, kp_ref, vp_ref, k_hbm, v_hbm, og_ref, from_ref,
                     x_ref, yp_ref, mod_ref, g2_ref, w_out_ref, w_up_ref, cw_ref, cb_ref, w_down_ref,
                     out_ref,
                     o_scr, halo_ref, act_ref, run_ref, acc_ref, kbuf, vbuf, sem,
                     *, tiles_per_batch, d_pool, d_ff):
    s = pl.program_id(0)
    n_tiles = pl.num_programs(0) - 1
    ta = jnp.minimum(s, n_tiles - 1)
    ta_in_batch = lax.rem(ta, tiles_per_batch)
    batch_a = lax.div(ta, tiles_per_batch)
    tf_in_batch = lax.rem(jnp.maximum(s - 1, 0), tiles_per_batch)
    tm = q_ref.shape[1]
    n_qblk = tm // Q_BLOCK
    n_cols = q_ref.shape[2] // LANES

    @pl.when(s == 0)
    def _():
        o_scr[...] = jnp.zeros_like(o_scr)

    @pl.when(tf_in_batch == 0)
    def _():
        halo_ref[...] = jnp.zeros_like(halo_ref)

    lane = lax.broadcasted_iota(jnp.int32, (1, LANES), 1)
    head0 = lane < HEAD_DIM
    prev_valid = ta_in_batch > 0

    def rows(ref, start, width, c):
        cols = slice(c * LANES, (c + 1) * LANES)
        if len(ref.shape) == 3:
            return ref[0, start:start + width, cols]
        return ref[start:start + width, cols]

    def stacked_q(r, c):
        q = q_ref[0, r * Q_BLOCK:(r + 1) * Q_BLOCK, c * LANES:(c + 1) * LANES]
        zq = jnp.zeros_like(q)
        return jnp.concatenate([jnp.where(head0, q, zq), jnp.where(head0, zq, q)], axis=0)

    def scores(c, q2, win):
        kb = rows(win.k, win.start, win.width, c)
        return lax.dot_general(q2, kb, (((1,), (1,)), ((), ())), preferred_element_type=f32)

    def log_terms(zn, win):
        width = zn.shape[1]
        log_1m = jnp.minimum(zn, 0.0) - jnp.log(1.0 + jnp.exp2(jnp.abs(zn) * -LOG2_E))
        mask = None
        if win.diag:
            row = lax.broadcasted_iota(jnp.int32, zn.shape, 0) & (Q_BLOCK - 1)
            mask = lax.broadcasted_iota(jnp.int32, zn.shape, 1) < row
        elif win.prev:
            mask = prev_valid
        if mask is not None:
            log_1m = jnp.where(mask, log_1m, 0.0)
        from_here = jnp.dot(log_1m.astype(bf16), from_ref[0:width, 0:width], preferred_element_type=f32)
        return from_here - zn, jnp.sum(log_1m, axis=1, keepdims=True), mask

    def weighted_values(c, win, log_w, mask, run):
        a = jnp.exp(log_w if run is None else log_w + run)
        if mask is not None:
            a = jnp.where(mask, a, 0.0)
        return jnp.dot(a.astype(bf16), rows(win.v, win.start, win.width, c), preferred_element_type=f32)

    def visit(r, windows, fresh):
        q2 = [stacked_q(r, c) for c in range(n_cols)]
        zs = [[scores(c, q2[c], w) for w in windows] for c in range(n_cols)]
        yield
        terms = [[log_terms(z, w) for z, w in zip(zs[c], windows)] for c in range(n_cols)]
        yield
        for c in range(n_cols):
            slot = r * n_cols + c
            run, acc = (None, None) if fresh else (run_ref[slot], acc_ref[slot])
            for w, (log_w, row_sum, mask) in zip(windows, terms[c]):
                pv = weighted_values(c, w, log_w, mask, run)
                acc = pv if acc is None else acc + pv
                run = row_sum if run is None else run + row_sum
            run_ref[slot] = run
            acc_ref[slot] = acc
            if fresh:
                normalise(r, c, acc)
        yield

    def normalise(r, c, acc):
        cols = slice(c * LANES, (c + 1) * LANES)
        o = jnp.where(head0, acc[0:Q_BLOCK, :], acc[Q_BLOCK:, :])
        o2 = o * o
        ssq0 = jnp.sum(jnp.where(head0, o2, 0.0), axis=-1, keepdims=True)
        ssq1 = jnp.sum(jnp.where(head0, 0.0, o2), axis=-1, keepdims=True)
        ms = jnp.where(head0, ssq0, ssq1) * (1.0 / HEAD_DIM)
        o_scr[r * Q_BLOCK:(r + 1) * Q_BLOCK, cols] = (o * lax.rsqrt(ms + EPS) * og_ref[:, cols]).astype(o_scr.dtype)

    def first_windows(r):
        wins = [_Window(kc_ref, vc_ref, r * Q_BLOCK, Q_BLOCK, True, False)]
        start = r * Q_BLOCK - K_WIDE
        if start >= 0:
            wins.append(_Window(kc_ref, vc_ref, start, K_WIDE, False, False))
        else:
            if r > 0:
                wins.append(_Window(kc_ref, vc_ref, 0, r * Q_BLOCK, False, False))
            wins.append(_Window(kp_ref, vp_ref, K_WIDE + start, -start, False, True))
        return wins

    def attention_main():
        blocks = [visit(r, first_windows(r), True) for r in range(n_qblk)]
        n_stages = 3
        for step in range(n_qblk + (n_stages - 1) * ATTN_SKEW):
            for r in reversed(range(n_qblk)):
                stage, phase = divmod(step - r, ATTN_SKEW)
                if phase == 0 and 0 <= stage < n_stages and step - r >= 0:
                    next(blocks[r])
                    yield

    gate1 = mod_ref[0, 2:3, :]
    shift = mod_ref[0, 3:4, :]
    scale = mod_ref[0, 4:5, :]
    gate2 = mod_ref[0, 5:6, :]
    n_chunks = d_ff // FF_CHUNK

    def ffn_main():
        mixed = (jnp.dot(yp_ref[0], w_out_ref[0:d_pool, :], preferred_element_type=f32)
                 + jnp.dot(o_scr[...], w_out_ref[d_pool:, :], preferred_element_type=f32))
        x1 = x_ref[0] + gate1 * mixed
        inv_rms = lax.rsqrt(jnp.mean(x1 * x1, axis=-1, keepdims=True) + EPS)
        hb = ((x1 * inv_rms) * (g2_ref[...] * (1.0 + scale)) + shift).astype(bf16)
        yield

        def conv(c, cols):
            up = jnp.dot(hb, w_up_ref[:, cols], preferred_element_type=f32)
            ext = jnp.concatenate([halo_ref[c], up], axis=0)
            halo_ref[c] = up[tm - CONV_HALO:, :]
            prev1 = pltpu.roll(ext, 1, axis=0)[CONV_HALO:, :]
            prev2 = pltpu.roll(ext, 2, axis=0)[CONV_HALO:, :]
            return (cw_ref[2:3, cols] * up + cw_ref[1:2, cols] * prev1
                    + cw_ref[0:1, cols] * prev2 + cb_ref[:, cols])

        for c in range(n_chunks):
            gate = conv(c, slice(c * FF_CHUNK, (c + 1) * FF_CHUNK))
            val = conv(n_chunks + c, slice(d_ff + c * FF_CHUNK, d_ff + (c + 1) * FF_CHUNK))
            act = gate / (1.0 + jnp.exp(-gate)) * val
            act_ref[:, c * FF_CHUNK:(c + 1) * FF_CHUNK] = act.astype(bf16)
            yield
        ffn = jnp.dot(act_ref[...], w_down_ref[...], preferred_element_type=f32)
        out_ref[0] = x1 + gate2 * ffn
        yield

    for _ in itertools.zip_longest(attention_main(), ffn_main()):
        pass

    def unfinished():
        return jnp.max(run_ref[...]) > LOG_WEIGHT_ZERO

    def drain(gen):
        for _ in gen:
            pass

    go = unfinished()

    @pl.when(go)
    def _():
        for r in range(1, n_qblk):
            wins = []
            seen = r * Q_BLOCK - K_WIDE
            if seen > 0:
                wins.append(_Window(kc_ref, vc_ref, 0, seen, False, False))
            top = K_WIDE + min(seen, 0)
            wins.append(_Window(kp_ref, vp_ref, 0, top, False, True))
            drain(visit(r, wins, False))

    def cond(state):
        jb, more = state
        return jnp.logical_and(jb >= 0, more)

    def body(state):
        jb, _ = state
        src = pl.ds(pl.multiple_of(jb * Q_BLOCK, Q_BLOCK), Q_BLOCK)
        copies = [pltpu.make_async_copy(k_hbm.at[batch_a, src, :], kbuf, sem.at[0]),
                  pltpu.make_async_copy(v_hbm.at[batch_a, src, :], vbuf, sem.at[1])]
        for cp in copies:
            cp.start()
        for cp in copies:
            cp.wait()
        for r in range(n_qblk):
            drain(visit(r, [_Window(kbuf, vbuf, 0, Q_BLOCK, False, False)], False))
        return jb - 1, unfinished()

    first = ta_in_batch * n_qblk - (K_WIDE // Q_BLOCK + 1)
    lax.while_loop(cond, body, (first, go))

    @pl.when(go)
    def _():
        for r in range(n_qblk):
            for c in range(n_cols):
                normalise(r, c, acc_ref[r * n_cols + c])


def _attn_ffn(x, yp, q, k, v, mod, out_g, g2, w_out, w_up, conv_w, conv_b, w_down, tm=512):
    bsz, s, d = x.shape
    d_pool = yp.shape[-1]
    d_attn = q.shape[-1]
    d_ff = w_down.shape[0]
    tpb = s // tm
    n_tiles = bsz * tpb
    n_slots = (tm // Q_BLOCK) * (d_attn // LANES)
    idx = lax.broadcasted_iota(jnp.int32, (K_WIDE, K_WIDE), 0)
    from_here = (idx >= idx.T).astype(bf16)

    def attn_tile(step):
        t = jnp.minimum(step, n_tiles - 1)
        return t // tpb, t % tpb

    def ffn_tile(step):
        t = jnp.maximum(step - 1, 0)
        return t // tpb, t % tpb

    def a_spec(width):
        return pl.BlockSpec((1, tm, width), lambda i: (*attn_tile(i), 0))

    def prev_spec(width):
        per_tile = tm // K_WIDE
        return pl.BlockSpec((1, K_WIDE, width),
                            lambda i: (attn_tile(i)[0], jnp.maximum(attn_tile(i)[1] * per_tile - 1, 0), 0))

    def f_spec(width):
        return pl.BlockSpec((1, tm, width), lambda i: (*ffn_tile(i), 0))

    hbm = pl.BlockSpec(memory_space=pl.ANY)
    return pl.pallas_call(
        functools.partial(_attn_ffn_kernel, tiles_per_batch=tpb, d_pool=d_pool, d_ff=d_ff),
        grid=(n_tiles + 1,),
        in_specs=[
            a_spec(d_attn), a_spec(d_attn), a_spec(d_attn), prev_spec(d_attn), prev_spec(d_attn), hbm, hbm,
            _const_spec(out_g.shape), _const_spec(from_here.shape),
            f_spec(d), f_spec(d_pool),
            pl.BlockSpec((1,) + mod.shape[1:], lambda i: (ffn_tile(i)[0], 0, 0)),
            _const_spec(g2.shape),
            _const_spec(w_out.shape),
            _const_spec(w_up.shape),
            _const_spec(conv_w.shape),
            _const_spec(conv_b.shape),
            _const_spec(w_down.shape),
        ],
        out_specs=f_spec(d),
        out_shape=jax.ShapeDtypeStruct((bsz, s, d), x.dtype),
        scratch_shapes=[
            pltpu.VMEM((tm, d_attn), bf16),
            pltpu.VMEM((2 * d_ff // FF_CHUNK, CONV_HALO, FF_CHUNK), f32),
            pltpu.VMEM((tm, d_ff), bf16),
            pltpu.VMEM((n_slots, 2 * Q_BLOCK, 1), f32),
            pltpu.VMEM((n_slots, 2 * Q_BLOCK, LANES), f32),
            pltpu.VMEM((Q_BLOCK, d_attn), bf16),
            pltpu.VMEM((Q_BLOCK, d_attn), bf16),
            pltpu.SemaphoreType.DMA((2,)),
        ],
        name="attn_convffn",
        compiler_params=pltpu.CompilerParams(
            dimension_semantics=("arbitrary",), vmem_limit_bytes=FUSED_VMEM_LIMIT),
    )(q, k, v, k, v, k, v, out_g, from_here, x, yp, mod, g2, w_out, w_up, conv_w, conv_b, w_down)


def kernel(x, c, ada_w, ada_b, norm1_g, w_in, pool_w, pool_b, pool_scale, q_norm_g, k_norm_g, attn_out_g,
           w_out, norm2_g, w_up, conv_w, conv_b, w_down):
    depth = ada_w.shape[0]
    d = x.shape[-1]
    n_heads = attn_out_g.shape[1]
    row = lambda a: a.reshape(1, -1)
    for l in range(depth):
        mod = _adaln(c, ada_w[l], ada_b[l]).reshape(c.shape[0], -1, d)
        yp, q, k, v = _inproj(
            x, mod, row(norm1_g[l]), w_in[l].astype(bf16), pool_w[l].astype(bf16), row(pool_b[l]),
            row(pool_scale[l]), row(jnp.tile(q_norm_g[l], n_heads)), row(jnp.tile(k_norm_g[l], n_heads)))
        x = _attn_ffn(x, yp, q, k, v, mod, row(attn_out_g[l]), row(norm2_g[l]), w_out[l].astype(bf16),
                      w_up[l].astype(bf16), conv_w[l], row(conv_b[l]), w_down[l].astype(bf16))
    return x
```

```python
import collections
import functools
import itertools
import math

import jax
import jax.numpy as jnp
from jax import lax
from jax.experimental import pallas as pl
from jax.experimental.pallas import tpu as pltpu

EPS = 1e-6
HEAD_DIM = 64
N_POOL_GROUPS = 4
POOL_WINDOWS = (2, 4, 8, 16)
POOL_HALO = 16
CONV_WIDTH = 3
CONV_HALO = 8
LANES = 128
BF16_SUBLANES = 16
Q_BLOCK = 128
K_WIDE = 256
FF_CHUNK = 256
ATTN_SKEW = 1
LOG_WEIGHT_ZERO = -104.7
LOG2_E = math.log2(math.e)
VMEM_LIMIT = 48 * 1024 * 1024
FUSED_VMEM_LIMIT = 58 * 1024 * 1024

bf16 = jnp.bfloat16
f32 = jnp.float32

_Window = collections.namedtuple("_Window", "k v start width diag prev")


def _const_spec(shape):
    nd = len(shape)
    return pl.BlockSpec(shape, lambda *_: (0,) * nd, pipeline_mode=pl.Buffered(1))


def _adaln_kernel(ct_ref, w_ref, b_ref, out_ref):
    ct = ct_ref[...]
    act = ct / (1.0 + jnp.exp(-ct))
    w = w_ref[...]
    rows = []
    for b in range(ct.shape[1]):
        rows.append(jnp.sum(act[:, b:b + 1] * w, axis=0, keepdims=True))
    out_ref[...] = jnp.concatenate(rows, axis=0) + b_ref[...]


def _adaln(c, ada_w, ada_b, tn=1536):
    bsz, d = c.shape
    n = ada_w.shape[1]
    return pl.pallas_call(
        _adaln_kernel,
        grid=(n // tn,),
        in_specs=[
            pl.BlockSpec((d, bsz), lambda j: (0, 0)),
            pl.BlockSpec((d, tn), lambda j: (0, j)),
            pl.BlockSpec((1, tn), lambda j: (0, j)),
        ],
        out_specs=pl.BlockSpec((bsz, tn), lambda j: (0, j)),
        out_shape=jax.ShapeDtypeStruct((bsz, n), f32),
        name="adaln",
        compiler_params=pltpu.CompilerParams(dimension_semantics=("arbitrary",)),
    )(c.T, ada_w, ada_b.reshape(1, n))


def _inproj_kernel(x_ref, mod_ref, g1_ref, w_in_ref, pool_w_ref, pool_b_ref, pool_s_ref,
                   qg_ref, kg_ref, *rest, d_pool, d_attn):
    n_cast = (len(rest) - 5) // 2
    cast_src = rest[:n_cast]
    yp_ref, q_ref, k_ref, v_ref = rest[n_cast:n_cast + 4]
    cast_dst = rest[n_cast + 4:2 * n_cast + 4]
    halo_ref = rest[-1]
    j = pl.program_id(1)
    tm = x_ref.shape[1]

    @pl.when(j == 0)
    def _():
        halo_ref[...] = jnp.zeros_like(halo_ref)

    x = x_ref[0]
    shift = mod_ref[0, 0:1, :]
    scale = mod_ref[0, 1:2, :]
    inv_rms = lax.rsqrt(jnp.mean(x * x, axis=-1, keepdims=True) + EPS)
    h = (x * inv_rms) * (g1_ref[...] * (1.0 + scale)) + shift
    hb = h.astype(bf16)

    project = lambda lo, hi: jnp.dot(hb, w_in_ref[:, lo:hi], preferred_element_type=f32)
    u = project(0, d_pool)
    tq = project(d_pool, d_pool + d_attn)
    tk = project(d_pool + d_attn, d_pool + 2 * d_attn)
    v_ref[0] = project(d_pool + 2 * d_attn, d_pool + 3 * d_attn).astype(v_ref.dtype)

    ext = jnp.concatenate([halo_ref[...], u], axis=0)
    halo_ref[...] = u[tm - POOL_HALO:, :]
    t1 = (j * tm + 1 + lax.broadcasted_iota(jnp.int32, (tm, 1), 0))
    ys = []
    for g, w in enumerate(POOL_WINDOWS):
        cols = slice(g * LANES, (g + 1) * LANES)
        s = ext[:, cols]
        span = 1
        while span < w:
            s = s + pltpu.roll(s, span, axis=0)
            span *= 2
        count = jnp.minimum(t1, w).astype(f32)
        pooled = s[POOL_HALO:, :] / count - u[:, cols]
        y = jnp.dot(pooled.astype(bf16), pool_w_ref[g], preferred_element_type=f32)
        ys.append((y + pool_b_ref[:, cols]) * pool_s_ref[:, cols])
    yp_ref[0] = jnp.concatenate(ys, axis=-1).astype(yp_ref.dtype)

    head0 = lax.broadcasted_iota(jnp.int32, (1, LANES), 1) < HEAD_DIM

    def head_norm(t, gain, post):
        inv = []
        for c in range(d_attn // LANES):
            sq = t[:, c * LANES:(c + 1) * LANES]
            sq = sq * sq
            ssq0 = jnp.sum(jnp.where(head0, sq, 0.0), axis=-1, keepdims=True)
            ssq1 = jnp.sum(jnp.where(head0, 0.0, sq), axis=-1, keepdims=True)
            inv.append(jnp.where(head0, lax.rsqrt(ssq0 * (1.0 / HEAD_DIM) + EPS),
                                 lax.rsqrt(ssq1 * (1.0 / HEAD_DIM) + EPS)))
        return (t * jnp.concatenate(inv, axis=-1)) * (gain * post)

    q_ref[0] = head_norm(tq, qg_ref[...], -1.0 / math.sqrt(HEAD_DIM)).astype(q_ref.dtype)
    k_ref[0] = head_norm(tk, kg_ref[...], 1.0).astype(k_ref.dtype)

    for src, dst in zip(cast_src, cast_dst):
        dst[...] = src[...].astype(dst.dtype)


def _inproj(x, mod, g1, w_in, pool_w, pool_b, pool_s, qg, kg, later_weights, tm=512):
    bsz, s, d = x.shape
    d_pool = pool_b.shape[-1]
    d_attn = (w_in.shape[1] - d_pool) // 3
    n_j = s // tm
    tile = lambda width: pl.BlockSpec((1, tm, width), lambda b, j: (b, j, 0))
    out = jax.ShapeDtypeStruct((bsz, s, d_attn), bf16)

    def row_slice(w):
        rows, cols = w.shape
        n_blocks = bsz * n_j
        while rows % (n_blocks * BF16_SUBLANES):
            assert n_blocks % 2 == 0, w.shape
            n_blocks //= 2
        return pl.BlockSpec((rows // n_blocks, cols), lambda b, j: (jnp.minimum(b * n_j + j, n_blocks - 1), 0))

    cast_specs = [row_slice(w) for w in later_weights]
    return pl.pallas_call(
        functools.partial(_inproj_kernel, d_pool=d_pool, d_attn=d_attn),
        grid=(bsz, n_j),
        in_specs=[
            tile(d),
            pl.BlockSpec((1,) + mod.shape[1:], lambda b, j: (b, 0, 0)),
            _const_spec(g1.shape),
            _const_spec(w_in.shape),
            _const_spec(pool_w.shape),
            _const_spec(pool_b.shape),
            _const_spec(pool_s.shape),
            _const_spec(qg.shape),
            _const_spec(kg.shape),
        ] + cast_specs,
        out_specs=[tile(d_pool), tile(d_attn), tile(d_attn), tile(d_attn)] + cast_specs,
        out_shape=[jax.ShapeDtypeStruct((bsz, s, d_pool), bf16), out, out, out]
        + [jax.ShapeDtypeStruct(w.shape, bf16) for w in later_weights],
        scratch_shapes=[pltpu.VMEM((POOL_HALO, d_pool), f32)],
        name="inproj",
        compiler_params=pltpu.CompilerParams(
            dimension_semantics=("arbitrary", "arbitrary"), vmem_limit_bytes=VMEM_LIMIT),
    )(x, mod, g1, w_in, pool_w, pool_b, pool_s, qg, kg, *later_weights)


def _attn_ffn_kernel(q_ref, kc_ref, vc_ref, kp_ref, vp_ref, k_hbm, v_hbm, og_ref, from_ref,
                     x_ref, yp_ref, mod_ref, g2_ref, w_out_ref, w_up_ref, cw_ref, cb_ref, w_down_ref,
                     out_ref,
                     o_scr, halo_ref, act_ref, run_ref, acc_ref, kbuf, vbuf, sem,
                     *, tiles_per_batch, d_pool, d_ff):
    s = pl.program_id(0)
    n_tiles = pl.num_programs(0) - 1
    ta = jnp.minimum(s, n_tiles - 1)
    ta_in_batch = lax.rem(ta, tiles_per_batch)
    batch_a = lax.div(ta, tiles_per_batch)
    tf_in_batch = lax.rem(jnp.maximum(s - 1, 0), tiles_per_batch)
    tm = q_ref.shape[1]
    n_qblk = tm // Q_BLOCK
    n_cols = q_ref.shape[2] // LANES

    @pl.when(s == 0)
    def _():
        o_scr[...] = jnp.zeros_like(o_scr)

    @pl.when(tf_in_batch == 0)
    def _():
        halo_ref[...] = jnp.zeros_like(halo_ref)

    lane = lax.broadcasted_iota(jnp.int32, (1, LANES), 1)
    head0 = lane < HEAD_DIM
    prev_valid = ta_in_batch > 0

    def rows(ref, start, width, c):
        cols = slice(c * LANES, (c + 1) * LANES)
        if len(ref.shape) == 3:
            return ref[0, start:start + width, cols]
        return ref[start:start + width, cols]

    def stacked_q(r, c):
        q = q_ref[0, r * Q_BLOCK:(r + 1) * Q_BLOCK, c * LANES:(c + 1) * LANES]
        zq = jnp.zeros_like(q)
        return jnp.concatenate([jnp.where(head0, q, zq), jnp.where(head0, zq, q)], axis=0)

    def scores(c, q2, win):
        kb = rows(win.k, win.start, win.width, c)
        return lax.dot_general(q2, kb, (((1,), (1,)), ((), ())), preferred_element_type=f32)

    def log_terms(zn, win):
        width = zn.shape[1]
        log_1m = jnp.minimum(zn, 0.0) - jnp.log(1.0 + jnp.exp2(jnp.abs(zn) * -LOG2_E))
        mask = None
        if win.diag:
            row = lax.broadcasted_iota(jnp.int32, zn.shape, 0) & (Q_BLOCK - 1)
            mask = lax.broadcasted_iota(jnp.int32, zn.shape, 1) < row
        elif win.prev:
            mask = prev_valid
        if mask is not None:
            log_1m = jnp.where(mask, log_1m, 0.0)
        from_here = jnp.dot(log_1m.astype(bf16), from_ref[0:width, 0:width], preferred_element_type=f32)
        return from_here - zn, jnp.sum(log_1m, axis=1, keepdims=True), mask

    def weighted_values(c, win, log_w, mask, run):
        a = jnp.exp(log_w if run is None else log_w + run)
        if mask is not None:
            a = jnp.where(mask, a, 0.0)
        return jnp.dot(a.astype(bf16), rows(win.v, win.start, win.width, c), preferred_element_type=f32)

    def visit(r, windows, fresh):
        q2 = [stacked_q(r, c) for c in range(n_cols)]
        zs = [[scores(c, q2[c], w) for w in windows] for c in range(n_cols)]
        yield
        terms = [[log_terms(z, w) for z, w in zip(zs[c], windows)] for c in range(n_cols)]
        yield
        for c in range(n_cols):
            slot = r * n_cols + c
            run, acc = (None, None) if fresh else (run_ref[slot], acc_ref[slot])
            for w, (log_w, row_sum, mask) in zip(windows, terms[c]):
                pv = weighted_values(c, w, log_w, mask, run)
                acc = pv if acc is None else acc + pv
                run = row_sum if run is None else run + row_sum
            run_ref[slot] = run
            acc_ref[slot] = acc
            if fresh:
                normalise(r, c, acc)
        yield

    def normalise(r, c, acc):
        cols = slice(c * LANES, (c + 1) * LANES)
        o = jnp.where(head0, acc[0:Q_BLOCK, :], acc[Q_BLOCK:, :])
        o2 = o * o
        ssq0 = jnp.sum(jnp.where(head0, o2, 0.0), axis=-1, keepdims=True)
        ssq1 = jnp.sum(jnp.where(head0, 0.0, o2), axis=-1, keepdims=True)
        ms = jnp.where(head0, ssq0, ssq1) * (1.0 / HEAD_DIM)
        o_scr[r * Q_BLOCK:(r + 1) * Q_BLOCK, cols] = (o * lax.rsqrt(ms + EPS) * og_ref[:, cols]).astype(o_scr.dtype)

    def first_windows(r):
        wins = [_Window(kc_ref, vc_ref, r * Q_BLOCK, Q_BLOCK, True, False)]
        start = r * Q_BLOCK - K_WIDE
        if start >= 0:
            wins.append(_Window(kc_ref, vc_ref, start, K_WIDE, False, False))
        else:
            if r > 0:
                wins.append(_Window(kc_ref, vc_ref, 0, r * Q_BLOCK, False, False))
            wins.append(_Window(kp_ref, vp_ref, K_WIDE + start, -start, False, True))
        return wins

    def attention_main():
        blocks = [visit(r, first_windows(r), True) for r in range(n_qblk)]
        n_stages = 3
        for step in range(n_qblk + (n_stages - 1) * ATTN_SKEW):
            for r in reversed(range(n_qblk)):
                stage, phase = divmod(step - r, ATTN_SKEW)
                if phase == 0 and 0 <= stage < n_stages and step - r >= 0:
                    next(blocks[r])
                    yield

    gate1 = mod_ref[0, 2:3, :]
    shift = mod_ref[0, 3:4, :]
    scale = mod_ref[0, 4:5, :]
    gate2 = mod_ref[0, 5:6, :]
    n_chunks = d_ff // FF_CHUNK

    def ffn_main():
        mixed = (jnp.dot(yp_ref[0], w_out_ref[0:d_pool, :], preferred_element_type=f32)
                 + jnp.dot(o_scr[...], w_out_ref[d_pool:, :], preferred_element_type=f32))
        x1 = x_ref[0] + gate1 * mixed
        inv_rms = lax.rsqrt(jnp.mean(x1 * x1, axis=-1, keepdims=True) + EPS)
        hb = ((x1 * inv_rms) * (g2_ref[...] * (1.0 + scale)) + shift).astype(bf16)
        yield

        def conv(c, cols):
            up = jnp.dot(hb, w_up_ref[:, cols], preferred_element_type=f32)
            ext = jnp.concatenate([halo_ref[c], up], axis=0)
            halo_ref[c] = up[tm - CONV_HALO:, :]
            prev1 = pltpu.roll(ext, 1, axis=0)[CONV_HALO:, :]
            prev2 = pltpu.roll(ext, 2, axis=0)[CONV_HALO:, :]
            return (cw_ref[2:3, cols] * up + cw_ref[1:2, cols] * prev1
                    + cw_ref[0:1, cols] * prev2 + cb_ref[:, cols])

        for c in range(n_chunks):
            gate = conv(c, slice(c * FF_CHUNK, (c + 1) * FF_CHUNK))
            val = conv(n_chunks + c, slice(d_ff + c * FF_CHUNK, d_ff + (c + 1) * FF_CHUNK))
            act = gate / (1.0 + jnp.exp2(gate * -LOG2_E)) * val
            act_ref[:, c * FF_CHUNK:(c + 1) * FF_CHUNK] = act.astype(bf16)
            yield
        ffn = jnp.dot(act_ref[...], w_down_ref[...], preferred_element_type=f32)
        out_ref[0] = x1 + gate2 * ffn
        yield

    for _ in itertools.zip_longest(attention_main(), ffn_main()):
        pass

    def unfinished():
        return jnp.max(run_ref[...]) > LOG_WEIGHT_ZERO

    def drain(gen):
        for _ in gen:
            pass

    go = unfinished()

    @pl.when(go)
    def _():
        for r in range(1, n_qblk):
            wins = []
            seen = r * Q_BLOCK - K_WIDE
            if seen > 0:
                wins.append(_Window(kc_ref, vc_ref, 0, seen, False, False))
            top = K_WIDE + min(seen, 0)
            wins.append(_Window(kp_ref, vp_ref, 0, top, False, True))
            drain(visit(r, wins, False))

    def cond(state):
        jb, more = state
        return jnp.logical_and(jb >= 0, more)

    def body(state):
        jb, _ = state
        src = pl.ds(pl.multiple_of(jb * Q_BLOCK, Q_BLOCK), Q_BLOCK)
        copies = [pltpu.make_async_copy(k_hbm.at[batch_a, src, :], kbuf, sem.at[0]),
                  pltpu.make_async_copy(v_hbm.at[batch_a, src, :], vbuf, sem.at[1])]
        for cp in copies:
            cp.start()
        for cp in copies:
            cp.wait()
        for r in range(n_qblk):
            drain(visit(r, [_Window(kbuf, vbuf, 0, Q_BLOCK, False, False)], False))
        return jb - 1, unfinished()

    first = ta_in_batch * n_qblk - (K_WIDE // Q_BLOCK + 1)
    lax.while_loop(cond, body, (first, go))

    @pl.when(go)
    def _():
        for r in range(n_qblk):
            for c in range(n_cols):
                normalise(r, c, acc_ref[r * n_cols + c])


def _attn_ffn(x, yp, q, k, v, mod, out_g, g2, w_out, w_up, conv_w, conv_b, w_down, tm=512):
    bsz, s, d = x.shape
    d_pool = yp.shape[-1]
    d_attn = q.shape[-1]
    d_ff = w_down.shape[0]
    tpb = s // tm
    n_tiles = bsz * tpb
    n_slots = (tm // Q_BLOCK) * (d_attn // LANES)
    idx = lax.broadcasted_iota(jnp.int32, (K_WIDE, K_WIDE), 0)
    from_here = (idx >= idx.T).astype(bf16)

    def attn_tile(step):
        t = jnp.minimum(step, n_tiles - 1)
        return t // tpb, t % tpb

    def ffn_tile(step):
        t = jnp.maximum(step - 1, 0)
        return t // tpb, t % tpb

    def a_spec(width):
        return pl.BlockSpec((1, tm, width), lambda i: (*attn_tile(i), 0))

    def prev_spec(width):
        per_tile = tm // K_WIDE
        return pl.BlockSpec((1, K_WIDE, width),
                            lambda i: (attn_tile(i)[0], jnp.maximum(attn_tile(i)[1] * per_tile - 1, 0), 0))

    def f_spec(width):
        return pl.BlockSpec((1, tm, width), lambda i: (*ffn_tile(i), 0))

    hbm = pl.BlockSpec(memory_space=pl.ANY)
    return pl.pallas_call(
        functools.partial(_attn_ffn_kernel, tiles_per_batch=tpb, d_pool=d_pool, d_ff=d_ff),
        grid=(n_tiles + 1,),
        in_specs=[
            a_spec(d_attn), a_spec(d_attn), a_spec(d_attn), prev_spec(d_attn), prev_spec(d_attn), hbm, hbm,
            _const_spec(out_g.shape), _const_spec(from_here.shape),
            f_spec(d), f_spec(d_pool),
            pl.BlockSpec((1,) + mod.shape[1:], lambda i: (ffn_tile(i)[0], 0, 0)),
            _const_spec(g2.shape),
            _const_spec(w_out.shape),
            _const_spec(w_up.shape),
            _const_spec(conv_w.shape),
            _const_spec(conv_b.shape),
            _const_spec(w_down.shape),
        ],
        out_specs=f_spec(d),
        out_shape=jax.ShapeDtypeStruct((bsz, s, d), x.dtype),
        scratch_shapes=[
            pltpu.VMEM((tm, d_attn), bf16),
            pltpu.VMEM((2 * d_ff // FF_CHUNK, CONV_HALO, FF_CHUNK), f32),
            pltpu.VMEM((tm, d_ff), bf16),
            pltpu.VMEM((n_slots, 2 * Q_BLOCK, 1), f32),
            pltpu.VMEM((n_slots, 2 * Q_BLOCK, LANES), f32),
            pltpu.VMEM((Q_BLOCK, d_attn), bf16),
            pltpu.VMEM((Q_BLOCK, d_attn), bf16),
            pltpu.SemaphoreType.DMA((2,)),
        ],
        name="attn_convffn",
        compiler_params=pltpu.CompilerParams(
            dimension_semantics=("arbitrary",), vmem_limit_bytes=FUSED_VMEM_LIMIT),
    )(q, k, v, k, v, k, v, out_g, from_here, x, yp, mod, g2, w_out, w_up, conv_w, conv_b, w_down)


def kernel(x, c, ada_w, ada_b, norm1_g, w_in, pool_w, pool_b, pool_scale, q_norm_g, k_norm_g, attn_out_g,
           w_out, norm2_g, w_up, conv_w, conv_b, w_down):
    depth = ada_w.shape[0]
    d = x.shape[-1]
    n_heads = attn_out_g.shape[1]
    row = lambda a: a.reshape(1, -1)
    for l in range(depth):
        mod = _adaln(c, ada_w[l], ada_b[l]).reshape(c.shape[0], -1, d)
        yp, q, k, v, w_out_b, w_up_b, w_down_b = _inproj(
            x, mod, row(norm1_g[l]), w_in[l].astype(bf16), pool_w[l].astype(bf16), row(pool_b[l]),
            row(pool_scale[l]), row(jnp.tile(q_norm_g[l], n_heads)), row(jnp.tile(k_norm_g[l], n_heads)),
            (w_out[l], w_up[l], w_down[l]))
        x = _attn_ffn(x, yp, q, k, v, mod, row(attn_out_g[l]), row(norm2_g[l]), w_out_b, w_up_b,
                      conv_w[l], row(conv_b[l]), w_down_b)
    return x
```

```python
import collections
import functools
import itertools
import math

import jax
import jax.numpy as jnp
from jax import lax
from jax.experimental import pallas as pl
from jax.experimental.pallas import tpu as pltpu

EPS = 1e-6
HEAD_DIM = 64
N_POOL_GROUPS = 4
POOL_WINDOWS = (2, 4, 8, 16)
POOL_HALO = 16
CONV_WIDTH = 3
CONV_HALO = 8
LANES = 128
BF16_SUBLANES = 16
Q_BLOCK = 128
K_WIDE = 256
FF_CHUNK = 256
ATTN_SKEW = 1
LOG_WEIGHT_ZERO = -104.7
LOG2_E = math.log2(math.e)
VMEM_LIMIT = 48 * 1024 * 1024
FUSED_VMEM_LIMIT = 58 * 1024 * 1024

bf16 = jnp.bfloat16
f32 = jnp.float32

_Window = collections.namedtuple("_Window", "k v start width diag prev")


def _const_spec(shape):
    nd = len(shape)
    return pl.BlockSpec(shape, lambda *_: (0,) * nd, pipeline_mode=pl.Buffered(1))


def _adaln_kernel(ct_ref, w_ref, b_ref, out_ref):
    ct = ct_ref[...]
    act = ct / (1.0 + jnp.exp(-ct))
    w = w_ref[...]
    rows = []
    for b in range(ct.shape[1]):
        rows.append(jnp.sum(act[:, b:b + 1] * w, axis=0, keepdims=True))
    out_ref[...] = jnp.concatenate(rows, axis=0) + b_ref[...]


def _adaln(c, ada_w, ada_b, tn=1536):
    bsz, d = c.shape
    n = ada_w.shape[1]
    return pl.pallas_call(
        _adaln_kernel,
        grid=(n // tn,),
        in_specs=[
            pl.BlockSpec((d, bsz), lambda j: (0, 0)),
            pl.BlockSpec((d, tn), lambda j: (0, j)),
            pl.BlockSpec((1, tn), lambda j: (0, j)),
        ],
        out_specs=pl.BlockSpec((bsz, tn), lambda j: (0, j)),
        out_shape=jax.ShapeDtypeStruct((bsz, n), f32),
        name="adaln",
        compiler_params=pltpu.CompilerParams(dimension_semantics=("arbitrary",)),
    )(c.T, ada_w, ada_b.reshape(1, n))


def _inproj_kernel(x_ref, mod_ref, g1_ref, w_in_ref, pool_w_ref, pool_b_ref, pool_s_ref,
                   qg_ref, kg_ref, *rest, d_pool, d_attn):
    n_cast = (len(rest) - 5) // 2
    cast_src = rest[:n_cast]
    yp_ref, q_ref, k_ref, v_ref = rest[n_cast:n_cast + 4]
    cast_dst = rest[n_cast + 4:2 * n_cast + 4]
    halo_ref = rest[-1]
    j = pl.program_id(1)
    tm = x_ref.shape[1]

    @pl.when(j == 0)
    def _():
        halo_ref[...] = jnp.zeros_like(halo_ref)

    x = x_ref[0]
    shift = mod_ref[0, 0:1, :]
    scale = mod_ref[0, 1:2, :]
    inv_rms = lax.rsqrt(jnp.mean(x * x, axis=-1, keepdims=True) + EPS)
    h = (x * inv_rms) * (g1_ref[...] * (1.0 + scale)) + shift
    hb = h.astype(bf16)

    project = lambda lo, hi: jnp.dot(hb, w_in_ref[:, lo:hi], preferred_element_type=f32)
    u = project(0, d_pool)
    tq = project(d_pool, d_pool + d_attn)
    tk = project(d_pool + d_attn, d_pool + 2 * d_attn)
    v_ref[0] = project(d_pool + 2 * d_attn, d_pool + 3 * d_attn).astype(v_ref.dtype)

    ext = jnp.concatenate([halo_ref[...], u], axis=0)
    halo_ref[...] = u[tm - POOL_HALO:, :]
    t1 = (j * tm + 1 + lax.broadcasted_iota(jnp.int32, (tm, 1), 0))
    ys = []
    for g, w in enumerate(POOL_WINDOWS):
        cols = slice(g * LANES, (g + 1) * LANES)
        s = ext[:, cols]
        span = 1
        while span < w:
            s = s + pltpu.roll(s, span, axis=0)
            span *= 2
        count = jnp.minimum(t1, w).astype(f32)
        pooled = s[POOL_HALO:, :] / count - u[:, cols]
        y = jnp.dot(pooled.astype(bf16), pool_w_ref[g], preferred_element_type=f32)
        ys.append((y + pool_b_ref[:, cols]) * pool_s_ref[:, cols])
    yp_ref[0] = jnp.concatenate(ys, axis=-1).astype(yp_ref.dtype)

    head0 = lax.broadcasted_iota(jnp.int32, (1, LANES), 1) < HEAD_DIM

    def head_norm(t, gain, post):
        inv = []
        for c in range(d_attn // LANES):
            sq = t[:, c * LANES:(c + 1) * LANES]
            sq = sq * sq
            ssq0 = jnp.sum(jnp.where(head0, sq, 0.0), axis=-1, keepdims=True)
            ssq1 = jnp.sum(jnp.where(head0, 0.0, sq), axis=-1, keepdims=True)
            inv.append(jnp.where(head0, lax.rsqrt(ssq0 * (1.0 / HEAD_DIM) + EPS),
                                 lax.rsqrt(ssq1 * (1.0 / HEAD_DIM) + EPS)))
        return (t * jnp.concatenate(inv, axis=-1)) * (gain * post)

    q_ref[0] = head_norm(tq, qg_ref[...], -1.0 / math.sqrt(HEAD_DIM)).astype(q_ref.dtype)
    k_ref[0] = head_norm(tk, kg_ref[...], 1.0).astype(k_ref.dtype)

    for src, dst in zip(cast_src, cast_dst):
        dst[...] = src[...].astype(dst.dtype)


def _inproj(x, mod, g1, w_in, pool_w, pool_b, pool_s, qg, kg, later_weights, tm=1024):
    bsz, s, d = x.shape
    d_pool = pool_b.shape[-1]
    d_attn = (w_in.shape[1] - d_pool) // 3
    n_j = s // tm
    tile = lambda width: pl.BlockSpec((1, tm, width), lambda b, j: (b, j, 0))
    out = jax.ShapeDtypeStruct((bsz, s, d_attn), bf16)

    def row_slice(w):
        rows, cols = w.shape
        n_blocks = bsz * n_j
        while rows % (n_blocks * BF16_SUBLANES):
            assert n_blocks % 2 == 0, w.shape
            n_blocks //= 2
        return pl.BlockSpec((rows // n_blocks, cols), lambda b, j: (jnp.minimum(b * n_j + j, n_blocks - 1), 0))

    cast_specs = [row_slice(w) for w in later_weights]
    return pl.pallas_call(
        functools.partial(_inproj_kernel, d_pool=d_pool, d_attn=d_attn),
        grid=(bsz, n_j),
        in_specs=[
            tile(d),
            pl.BlockSpec((1,) + mod.shape[1:], lambda b, j: (b, 0, 0)),
            _const_spec(g1.shape),
            _const_spec(w_in.shape),
            _const_spec(pool_w.shape),
            _const_spec(pool_b.shape),
            _const_spec(pool_s.shape),
            _const_spec(qg.shape),
            _const_spec(kg.shape),
        ] + cast_specs,
        out_specs=[tile(d_pool), tile(d_attn), tile(d_attn), tile(d_attn)] + cast_specs,
        out_shape=[jax.ShapeDtypeStruct((bsz, s, d_pool), bf16), out, out, out]
        + [jax.ShapeDtypeStruct(w.shape, bf16) for w in later_weights],
        scratch_shapes=[pltpu.VMEM((POOL_HALO, d_pool), f32)],
        name="inproj",
        compiler_params=pltpu.CompilerParams(
            dimension_semantics=("arbitrary", "arbitrary"), vmem_limit_bytes=VMEM_LIMIT),
    )(x, mod, g1, w_in, pool_w, pool_b, pool_s, qg, kg, *later_weights)


def _attn_ffn_kernel(q_ref, kc_ref, vc_ref, kp_ref, vp_ref, k_hbm, v_hbm, og_ref, from_ref,
                     x_ref, yp_ref, mod_ref, g2_ref, w_out_ref, w_up_ref, cw_ref, cb_ref, w_down_ref,
                     out_ref,
                     o_scr, halo_ref, act_ref, run_ref, acc_ref, kbuf, vbuf, sem,
                     *, tiles_per_batch, d_pool, d_ff):
    s = pl.program_id(0)
    n_tiles = pl.num_programs(0) - 1
    ta = jnp.minimum(s, n_tiles - 1)
    ta_in_batch = lax.rem(ta, tiles_per_batch)
    batch_a = lax.div(ta, tiles_per_batch)
    tf_in_batch = lax.rem(jnp.maximum(s - 1, 0), tiles_per_batch)
    tm = q_ref.shape[1]
    n_qblk = tm // Q_BLOCK
    n_cols = q_ref.shape[2] // LANES

    @pl.when(s == 0)
    def _():
        o_scr[...] = jnp.zeros_like(o_scr)

    @pl.when(tf_in_batch == 0)
    def _():
        halo_ref[...] = jnp.zeros_like(halo_ref)

    lane = lax.broadcasted_iota(jnp.int32, (1, LANES), 1)
    head0 = lane < HEAD_DIM
    prev_valid = ta_in_batch > 0

    def rows(ref, start, width, c):
        cols = slice(c * LANES, (c + 1) * LANES)
        if len(ref.shape) == 3:
            return ref[0, start:start + width, cols]
        return ref[start:start + width, cols]

    def stacked_q(r, c):
        q = q_ref[0, r * Q_BLOCK:(r + 1) * Q_BLOCK, c * LANES:(c + 1) * LANES]
        zq = jnp.zeros_like(q)
        return jnp.concatenate([jnp.where(head0, q, zq), jnp.where(head0, zq, q)], axis=0)

    def scores(c, q2, win):
        kb = rows(win.k, win.start, win.width, c)
        return lax.dot_general(q2, kb, (((1,), (1,)), ((), ())), preferred_element_type=f32)

    def log_terms(zn, win):
        width = zn.shape[1]
        log_1m = jnp.minimum(zn, 0.0) - jnp.log(1.0 + jnp.exp2(jnp.abs(zn) * -LOG2_E))
        mask = None
        if win.diag:
            row = lax.broadcasted_iota(jnp.int32, zn.shape, 0) & (Q_BLOCK - 1)
            mask = lax.broadcasted_iota(jnp.int32, zn.shape, 1) < row
        elif win.prev:
            mask = prev_valid
        if mask is not None:
            log_1m = jnp.where(mask, log_1m, 0.0)
        from_here = jnp.dot(log_1m.astype(bf16), from_ref[0:width, 0:width], preferred_element_type=f32)
        return from_here - zn, jnp.sum(log_1m, axis=1, keepdims=True), mask

    def weighted_values(c, win, log_w, mask, run):
        a = jnp.exp(log_w if run is None else log_w + run)
        if mask is not None:
            a = jnp.where(mask, a, 0.0)
        return jnp.dot(a.astype(bf16), rows(win.v, win.start, win.width, c), preferred_element_type=f32)

    def visit(r, windows, fresh):
        q2 = [stacked_q(r, c) for c in range(n_cols)]
        zs = [[scores(c, q2[c], w) for w in windows] for c in range(n_cols)]
        yield
        terms = [[log_terms(z, w) for z, w in zip(zs[c], windows)] for c in range(n_cols)]
        yield
        for c in range(n_cols):
            slot = r * n_cols + c
            run, acc = (None, None) if fresh else (run_ref[slot], acc_ref[slot])
            for w, (log_w, row_sum, mask) in zip(windows, terms[c]):
                pv = weighted_values(c, w, log_w, mask, run)
                acc = pv if acc is None else acc + pv
                run = row_sum if run is None else run + row_sum
            run_ref[slot] = run
            acc_ref[slot] = acc
            if fresh:
                normalise(r, c, acc)
        yield

    def normalise(r, c, acc):
        cols = slice(c * LANES, (c + 1) * LANES)
        o = jnp.where(head0, acc[0:Q_BLOCK, :], acc[Q_BLOCK:, :])
        o2 = o * o
        ssq0 = jnp.sum(jnp.where(head0, o2, 0.0), axis=-1, keepdims=True)
        ssq1 = jnp.sum(jnp.where(head0, 0.0, o2), axis=-1, keepdims=True)
        ms = jnp.where(head0, ssq0, ssq1) * (1.0 / HEAD_DIM)
        o_scr[r * Q_BLOCK:(r + 1) * Q_BLOCK, cols] = (o * lax.rsqrt(ms + EPS) * og_ref[:, cols]).astype(o_scr.dtype)

    def first_windows(r):
        wins = [_Window(kc_ref, vc_ref, r * Q_BLOCK, Q_BLOCK, True, False)]
        start = r * Q_BLOCK - K_WIDE
        if start >= 0:
            wins.append(_Window(kc_ref, vc_ref, start, K_WIDE, False, False))
        else:
            if r > 0:
                wins.append(_Window(kc_ref, vc_ref, 0, r * Q_BLOCK, False, False))
            wins.append(_Window(kp_ref, vp_ref, K_WIDE + start, -start, False, True))
        return wins

    def attention_main():
        blocks = [visit(r, first_windows(r), True) for r in range(n_qblk)]
        n_stages = 3
        for step in range(n_qblk + (n_stages - 1) * ATTN_SKEW):
            for r in reversed(range(n_qblk)):
                stage, phase = divmod(step - r, ATTN_SKEW)
                if phase == 0 and 0 <= stage < n_stages and step - r >= 0:
                    next(blocks[r])
                    yield

    gate1 = mod_ref[0, 2:3, :]
    shift = mod_ref[0, 3:4, :]
    scale = mod_ref[0, 4:5, :]
    gate2 = mod_ref[0, 5:6, :]
    n_chunks = d_ff // FF_CHUNK

    def ffn_main():
        mixed = (jnp.dot(yp_ref[0], w_out_ref[0:d_pool, :], preferred_element_type=f32)
                 + jnp.dot(o_scr[...], w_out_ref[d_pool:, :], preferred_element_type=f32))
        x1 = x_ref[0] + gate1 * mixed
        inv_rms = lax.rsqrt(jnp.mean(x1 * x1, axis=-1, keepdims=True) + EPS)
        hb = ((x1 * inv_rms) * (g2_ref[...] * (1.0 + scale)) + shift).astype(bf16)
        yield

        def conv(c, cols):
            up = jnp.dot(hb, w_up_ref[:, cols], preferred_element_type=f32)
            ext = jnp.concatenate([halo_ref[c], up], axis=0)
            halo_ref[c] = up[tm - CONV_HALO:, :]
            prev1 = pltpu.roll(ext, 1, axis=0)[CONV_HALO:, :]
            prev2 = pltpu.roll(ext, 2, axis=0)[CONV_HALO:, :]
            return (cw_ref[2:3, cols] * up + cw_ref[1:2, cols] * prev1
                    + cw_ref[0:1, cols] * prev2 + cb_ref[:, cols])

        for c in range(n_chunks):
            gate = conv(c, slice(c * FF_CHUNK, (c + 1) * FF_CHUNK))
            val = conv(n_chunks + c, slice(d_ff + c * FF_CHUNK, d_ff + (c + 1) * FF_CHUNK))
            act = gate / (1.0 + jnp.exp2(gate * -LOG2_E)) * val
            act_ref[:, c * FF_CHUNK:(c + 1) * FF_CHUNK] = act.astype(bf16)
            yield
        ffn = jnp.dot(act_ref[...], w_down_ref[...], preferred_element_type=f32)
        out_ref[0] = x1 + gate2 * ffn
        yield

    for _ in itertools.zip_longest(attention_main(), ffn_main()):
        pass

    def unfinished():
        return jnp.max(run_ref[...]) > LOG_WEIGHT_ZERO

    def drain(gen):
        for _ in gen:
            pass

    go = unfinished()

    @pl.when(go)
    def _():
        for r in range(1, n_qblk):
            wins = []
            seen = r * Q_BLOCK - K_WIDE
            if seen > 0:
                wins.append(_Window(kc_ref, vc_ref, 0, seen, False, False))
            top = K_WIDE + min(seen, 0)
            wins.append(_Window(kp_ref, vp_ref, 0, top, False, True))
            drain(visit(r, wins, False))

    def cond(state):
        jb, more = state
        return jnp.logical_and(jb >= 0, more)

    def body(state):
        jb, _ = state
        src = pl.ds(pl.multiple_of(jb * Q_BLOCK, Q_BLOCK), Q_BLOCK)
        copies = [pltpu.make_async_copy(k_hbm.at[batch_a, src, :], kbuf, sem.at[0]),
                  pltpu.make_async_copy(v_hbm.at[batch_a, src, :], vbuf, sem.at[1])]
        for cp in copies:
            cp.start()
        for cp in copies:
            cp.wait()
        for r in range(n_qblk):
            drain(visit(r, [_Window(kbuf, vbuf, 0, Q_BLOCK, False, False)], False))
        return jb - 1, unfinished()

    first = ta_in_batch * n_qblk - (K_WIDE // Q_BLOCK + 1)
    lax.while_loop(cond, body, (first, go))

    @pl.when(go)
    def _():
        for r in range(n_qblk):
            for c in range(n_cols):
                normalise(r, c, acc_ref[r * n_cols + c])


def _attn_ffn(x, yp, q, k, v, mod, out_g, g2, w_out, w_up, conv_w, conv_b, w_down, tm=512):
    bsz, s, d = x.shape
    d_pool = yp.shape[-1]
    d_attn = q.shape[-1]
    d_ff = w_down.shape[0]
    tpb = s // tm
    n_tiles = bsz * tpb
    n_slots = (tm // Q_BLOCK) * (d_attn // LANES)
    idx = lax.broadcasted_iota(jnp.int32, (K_WIDE, K_WIDE), 0)
    from_here = (idx >= idx.T).astype(bf16)

    def attn_tile(step):
        t = jnp.minimum(step, n_tiles - 1)
        return t // tpb, t % tpb

    def ffn_tile(step):
        t = jnp.maximum(step - 1, 0)
        return t // tpb, t % tpb

    def a_spec(width):
        return pl.BlockSpec((1, tm, width), lambda i: (*attn_tile(i), 0))

    def prev_spec(width):
        per_tile = tm // K_WIDE
        return pl.BlockSpec((1, K_WIDE, width),
                            lambda i: (attn_tile(i)[0], jnp.maximum(attn_tile(i)[1] * per_tile - 1, 0), 0))

    def f_spec(width):
        return pl.BlockSpec((1, tm, width), lambda i: (*ffn_tile(i), 0))

    hbm = pl.BlockSpec(memory_space=pl.ANY)
    return pl.pallas_call(
        functools.partial(_attn_ffn_kernel, tiles_per_batch=tpb, d_pool=d_pool, d_ff=d_ff),
        grid=(n_tiles + 1,),
        in_specs=[
            a_spec(d_attn), a_spec(d_attn), a_spec(d_attn), prev_spec(d_attn), prev_spec(d_attn), hbm, hbm,
            _const_spec(out_g.shape), _const_spec(from_here.shape),
            f_spec(d), f_spec(d_pool),
            pl.BlockSpec((1,) + mod.shape[1:], lambda i: (ffn_tile(i)[0], 0, 0)),
            _const_spec(g2.shape),
            _const_spec(w_out.shape),
            _const_spec(w_up.shape),
            _const_spec(conv_w.shape),
            _const_spec(conv_b.shape),
            _const_spec(w_down.shape),
        ],
        out_specs=f_spec(d),
        out_shape=jax.ShapeDtypeStruct((bsz, s, d), x.dtype),
        scratch_shapes=[
            pltpu.VMEM((tm, d_attn), bf16),
            pltpu.VMEM((2 * d_ff // FF_CHUNK, CONV_HALO, FF_CHUNK), f32),
            pltpu.VMEM((tm, d_ff), bf16),
            pltpu.VMEM((n_slots, 2 * Q_BLOCK, 1), f32),
            pltpu.VMEM((n_slots, 2 * Q_BLOCK, LANES), f32),
            pltpu.VMEM((Q_BLOCK, d_attn), bf16),
            pltpu.VMEM((Q_BLOCK, d_attn), bf16),
            pltpu.SemaphoreType.DMA((2,)),
        ],
        name="attn_convffn",
        compiler_params=pltpu.CompilerParams(
            dimension_semantics=("arbitrary",), vmem_limit_bytes=FUSED_VMEM_LIMIT),
    )(q, k, v, k, v, k, v, out_g, from_here, x, yp, mod, g2, w_out, w_up, conv_w, conv_b, w_down)


def kernel(x, c, ada_w, ada_b, norm1_g, w_in, pool_w, pool_b, pool_scale, q_norm_g, k_norm_g, attn_out_g,
           w_out, norm2_g, w_up, conv_w, conv_b, w_down):
    depth = ada_w.shape[0]
    d = x.shape[-1]
    n_heads = attn_out_g.shape[1]
    row = lambda a: a.reshape(1, -1)
    for l in range(depth):
        mod = _adaln(c, ada_w[l], ada_b[l]).reshape(c.shape[0], -1, d)
        yp, q, k, v, w_out_b, w_up_b, w_down_b = _inproj(
            x, mod, row(norm1_g[l]), w_in[l].astype(bf16), pool_w[l].astype(bf16), row(pool_b[l]),
            row(pool_scale[l]), row(jnp.tile(q_norm_g[l], n_heads)), row(jnp.tile(k_norm_g[l], n_heads)),
            (w_out[l], w_up[l], w_down[l]))
        x = _attn_ffn(x, yp, q, k, v, mod, row(attn_out_g[l]), row(norm2_g[l]), w_out_b, w_up_b,
                      conv_w[l], row(conv_b[l]), w_down_b)
    return x
```

```python
import collections
import functools
import itertools
import math

import jax
import jax.numpy as jnp
from jax import lax
from jax.experimental import pallas as pl
from jax.experimental.pallas import tpu as pltpu

EPS = 1e-6
HEAD_DIM = 64
N_POOL_GROUPS = 4
POOL_WINDOWS = (2, 4, 8, 16)
POOL_HALO = 16
CONV_WIDTH = 3
CONV_HALO = 8
LANES = 128
BF16_SUBLANES = 16
Q_BLOCK = 128
K_WIDE = 256
FF_CHUNK = 256
ATTN_SKEW = 1
LOG_WEIGHT_ZERO = -104.7
LOG2_E = math.log2(math.e)
VMEM_LIMIT = 48 * 1024 * 1024
FUSED_VMEM_LIMIT = 58 * 1024 * 1024

bf16 = jnp.bfloat16
f32 = jnp.float32

_Window = collections.namedtuple("_Window", "k v start width diag prev")


def _const_spec(shape):
    nd = len(shape)
    return pl.BlockSpec(shape, lambda *_: (0,) * nd, pipeline_mode=pl.Buffered(1))


def _adaln_kernel(ct_ref, w_ref, b_ref, *rest):
    n_cast = (len(rest) - 1) // 2
    cast_src, out_ref, cast_dst = rest[:n_cast], rest[n_cast], rest[n_cast + 1:]

    @pl.when(pl.program_id(0) == 0)
    def _():
        out_ref[...] = jnp.broadcast_to(b_ref[...], out_ref.shape)

    ct = ct_ref[...]
    act = ct / (1.0 + jnp.exp(-ct))
    w = w_ref[...]
    rows = []
    for b in range(ct.shape[1]):
        rows.append(jnp.sum(act[:, b:b + 1] * w, axis=0, keepdims=True))
    out_ref[...] += jnp.concatenate(rows, axis=0)

    for src, dst in zip(cast_src, cast_dst):
        dst[...] = src[...].astype(dst.dtype)


def _adaln(c, ada_w, ada_b, later_weights, tk=256):
    bsz, d = c.shape
    n = ada_w.shape[1]
    steps = d // tk

    def row_slice(w):
        rows, cols = w.shape
        assert rows % (steps * BF16_SUBLANES) == 0, w.shape
        return pl.BlockSpec((rows // steps, cols), lambda k: (k, 0))

    cast_specs = [row_slice(w) for w in later_weights]
    return pl.pallas_call(
        _adaln_kernel,
        grid=(steps,),
        in_specs=[
            pl.BlockSpec((tk, bsz), lambda k: (k, 0)),
            pl.BlockSpec((tk, n), lambda k: (k, 0)),
            pl.BlockSpec((1, n), lambda k: (0, 0)),
        ] + cast_specs,
        out_specs=[pl.BlockSpec((bsz, n), lambda k: (0, 0))] + cast_specs,
        out_shape=[jax.ShapeDtypeStruct((bsz, n), f32)]
        + [jax.ShapeDtypeStruct(w.shape, bf16) for w in later_weights],
        name="adaln",
        compiler_params=pltpu.CompilerParams(dimension_semantics=("arbitrary",)),
    )(c.T, ada_w, ada_b.reshape(1, n), *later_weights)


def _inproj_kernel(x_ref, mod_ref, g1_ref, w_in_ref, pool_w_ref, pool_b_ref, pool_s_ref,
                   qg_ref, kg_ref, *rest, d_pool, d_attn):
    n_cast = (len(rest) - 5) // 2
    cast_src = rest[:n_cast]
    yp_ref, q_ref, k_ref, v_ref = rest[n_cast:n_cast + 4]
    cast_dst = rest[n_cast + 4:2 * n_cast + 4]
    halo_ref = rest[-1]
    j = pl.program_id(1)
    tm = x_ref.shape[1]

    @pl.when(j == 0)
    def _():
        halo_ref[...] = jnp.zeros_like(halo_ref)

    x = x_ref[0]
    shift = mod_ref[0, 0:1, :]
    scale = mod_ref[0, 1:2, :]
    inv_rms = lax.rsqrt(jnp.mean(x * x, axis=-1, keepdims=True) + EPS)
    h = (x * inv_rms) * (g1_ref[...] * (1.0 + scale)) + shift
    hb = h.astype(bf16)

    project = lambda lo, hi: jnp.dot(hb, w_in_ref[:, lo:hi], preferred_element_type=f32)
    u = project(0, d_pool)
    tq = project(d_pool, d_pool + d_attn)
    tk = project(d_pool + d_attn, d_pool + 2 * d_attn)
    v_ref[0] = project(d_pool + 2 * d_attn, d_pool + 3 * d_attn).astype(v_ref.dtype)

    ext = jnp.concatenate([halo_ref[...], u], axis=0)
    halo_ref[...] = u[tm - POOL_HALO:, :]
    t1 = (j * tm + 1 + lax.broadcasted_iota(jnp.int32, (tm, 1), 0))
    ys = []
    for g, w in enumerate(POOL_WINDOWS):
        cols = slice(g * LANES, (g + 1) * LANES)
        s = ext[:, cols]
        span = 1
        while span < w:
            s = s + pltpu.roll(s, span, axis=0)
            span *= 2
        count = jnp.minimum(t1, w).astype(f32)
        pooled = s[POOL_HALO:, :] / count - u[:, cols]
        y = jnp.dot(pooled.astype(bf16), pool_w_ref[g], preferred_element_type=f32)
        ys.append((y + pool_b_ref[:, cols]) * pool_s_ref[:, cols])
    yp_ref[0] = jnp.concatenate(ys, axis=-1).astype(yp_ref.dtype)

    head0 = lax.broadcasted_iota(jnp.int32, (1, LANES), 1) < HEAD_DIM

    def head_norm(t, g_ref, post):
        gain = jnp.concatenate([g_ref[...], g_ref[...]], axis=-1) * post
        cols = []
        for c in range(d_attn // LANES):
            tc = t[:, c * LANES:(c + 1) * LANES]
            sq = tc * tc
            ssq0 = jnp.sum(jnp.where(head0, sq, 0.0), axis=-1, keepdims=True)
            ssq1 = jnp.sum(jnp.where(head0, 0.0, sq), axis=-1, keepdims=True)
            inv = jnp.where(head0, lax.rsqrt(ssq0 * (1.0 / HEAD_DIM) + EPS),
                            lax.rsqrt(ssq1 * (1.0 / HEAD_DIM) + EPS))
            cols.append((tc * inv) * gain)
        return jnp.concatenate(cols, axis=-1)

    q_ref[0] = head_norm(tq, qg_ref, -1.0 / math.sqrt(HEAD_DIM)).astype(q_ref.dtype)
    k_ref[0] = head_norm(tk, kg_ref, 1.0).astype(k_ref.dtype)

    for src, dst in zip(cast_src, cast_dst):
        dst[...] = src[...].astype(dst.dtype)


def _inproj(x, mod, g1, w_in, pool_w, pool_b, pool_s, qg, kg, later_weights, tm=1024):
    bsz, s, d = x.shape
    d_pool = pool_b.shape[-1]
    d_attn = (w_in.shape[1] - d_pool) // 3
    n_j = s // tm
    tile = lambda width: pl.BlockSpec((1, tm, width), lambda b, j: (b, j, 0))
    out = jax.ShapeDtypeStruct((bsz, s, d_attn), bf16)

    def row_slice(w):
        rows, cols = w.shape
        n_blocks = bsz * n_j
        while rows % (n_blocks * BF16_SUBLANES):
            assert n_blocks % 2 == 0, w.shape
            n_blocks //= 2
        return pl.BlockSpec((rows // n_blocks, cols), lambda b, j: (jnp.minimum(b * n_j + j, n_blocks - 1), 0))

    cast_specs = [row_slice(w) for w in later_weights]
    return pl.pallas_call(
        functools.partial(_inproj_kernel, d_pool=d_pool, d_attn=d_attn),
        grid=(bsz, n_j),
        in_specs=[
            tile(d),
            pl.BlockSpec((1,) + mod.shape[1:], lambda b, j: (b, 0, 0)),
            _const_spec(g1.shape),
            _const_spec(w_in.shape),
            _const_spec(pool_w.shape),
            _const_spec(pool_b.shape),
            _const_spec(pool_s.shape),
            _const_spec(qg.shape),
            _const_spec(kg.shape),
        ] + cast_specs,
        out_specs=[tile(d_pool), tile(d_attn), tile(d_attn), tile(d_attn)] + cast_specs,
        out_shape=[jax.ShapeDtypeStruct((bsz, s, d_pool), bf16), out, out, out]
        + [jax.ShapeDtypeStruct(w.shape, bf16) for w in later_weights],
        scratch_shapes=[pltpu.VMEM((POOL_HALO, d_pool), f32)],
        name="inproj",
        compiler_params=pltpu.CompilerParams(
            dimension_semantics=("arbitrary", "arbitrary"), vmem_limit_bytes=VMEM_LIMIT),
    )(x, mod, g1, w_in, pool_w, pool_b, pool_s, qg, kg, *later_weights)


def _attn_ffn_kernel(q_ref, kc_ref, vc_ref, kp_ref, vp_ref, k_hbm, v_hbm, og_ref, from_ref,
                     x_ref, yp_ref, mod_ref, g2_ref, w_out_ref, w_up_ref, cw_ref, cb_ref, w_down_ref,
                     out_ref,
                     o_scr, halo_ref, act_ref, run_ref, acc_ref, kbuf, vbuf, sem,
                     *, tiles_per_batch, d_pool, d_ff):
    s = pl.program_id(0)
    n_tiles = pl.num_programs(0) - 1
    ta = jnp.minimum(s, n_tiles - 1)
    ta_in_batch = lax.rem(ta, tiles_per_batch)
    batch_a = lax.div(ta, tiles_per_batch)
    tf_in_batch = lax.rem(jnp.maximum(s - 1, 0), tiles_per_batch)
    tm = q_ref.shape[1]
    n_qblk = tm // Q_BLOCK
    n_cols = q_ref.shape[2] // LANES

    @pl.when(s == 0)
    def _():
        o_scr[...] = jnp.zeros_like(o_scr)

    @pl.when(tf_in_batch == 0)
    def _():
        halo_ref[...] = jnp.zeros_like(halo_ref)

    lane = lax.broadcasted_iota(jnp.int32, (1, LANES), 1)
    head0 = lane < HEAD_DIM
    prev_valid = ta_in_batch > 0

    def rows(ref, start, width, c):
        cols = slice(c * LANES, (c + 1) * LANES)
        if len(ref.shape) == 3:
            return ref[0, start:start + width, cols]
        return ref[start:start + width, cols]

    def stacked_q(r, c):
        q = q_ref[0, r * Q_BLOCK:(r + 1) * Q_BLOCK, c * LANES:(c + 1) * LANES]
        zq = jnp.zeros_like(q)
        return jnp.concatenate([jnp.where(head0, q, zq), jnp.where(head0, zq, q)], axis=0)

    def scores(c, q2, win):
        kb = rows(win.k, win.start, win.width, c)
        return lax.dot_general(q2, kb, (((1,), (1,)), ((), ())), preferred_element_type=f32)

    def log_terms(zn, win):
        width = zn.shape[1]
        log_1m = jnp.minimum(zn, 0.0) - jnp.log(1.0 + jnp.exp2(jnp.abs(zn) * -LOG2_E))
        mask = None
        if win.diag:
            row = lax.broadcasted_iota(jnp.int32, zn.shape, 0) & (Q_BLOCK - 1)
            mask = lax.broadcasted_iota(jnp.int32, zn.shape, 1) < row
        elif win.prev:
            mask = prev_valid
        if mask is not None:
            log_1m = jnp.where(mask, log_1m, 0.0)
        from_here = jnp.dot(log_1m.astype(bf16), from_ref[0:width, 0:width], preferred_element_type=f32)
        return from_here - zn, jnp.sum(log_1m, axis=1, keepdims=True), mask

    def weighted_values(c, win, log_w, mask, run):
        a = jnp.exp(log_w if run is None else log_w + run)
        if mask is not None:
            a = jnp.where(mask, a, 0.0)
        return jnp.dot(a.astype(bf16), rows(win.v, win.start, win.width, c), preferred_element_type=f32)

    def visit(r, windows, fresh):
        q2 = [stacked_q(r, c) for c in range(n_cols)]
        zs = [[scores(c, q2[c], w) for w in windows] for c in range(n_cols)]
        yield
        terms = [[log_terms(z, w) for z, w in zip(zs[c], windows)] for c in range(n_cols)]
        yield
        for c in range(n_cols):
            slot = r * n_cols + c
            run, acc = (None, None) if fresh else (run_ref[slot], acc_ref[slot])
            for w, (log_w, row_sum, mask) in zip(windows, terms[c]):
                pv = weighted_values(c, w, log_w, mask, run)
                acc = pv if acc is None else acc + pv
                run = row_sum if run is None else run + row_sum
            run_ref[slot] = run
            acc_ref[slot] = acc
            if fresh:
                normalise(r, c, acc)
        yield

    def normalise(r, c, acc):
        cols = slice(c * LANES, (c + 1) * LANES)
        o = jnp.where(head0, acc[0:Q_BLOCK, :], acc[Q_BLOCK:, :])
        o2 = o * o
        ssq0 = jnp.sum(jnp.where(head0, o2, 0.0), axis=-1, keepdims=True)
        ssq1 = jnp.sum(jnp.where(head0, 0.0, o2), axis=-1, keepdims=True)
        ms = jnp.where(head0, ssq0, ssq1) * (1.0 / HEAD_DIM)
        o_scr[r * Q_BLOCK:(r + 1) * Q_BLOCK, cols] = (o * lax.rsqrt(ms + EPS) * og_ref[:, cols]).astype(o_scr.dtype)

    def first_windows(r):
        wins = [_Window(kc_ref, vc_ref, r * Q_BLOCK, Q_BLOCK, True, False)]
        start = r * Q_BLOCK - K_WIDE
        if start >= 0:
            wins.append(_Window(kc_ref, vc_ref, start, K_WIDE, False, False))
        else:
            if r > 0:
                wins.append(_Window(kc_ref, vc_ref, 0, r * Q_BLOCK, False, False))
            wins.append(_Window(kp_ref, vp_ref, K_WIDE + start, -start, False, True))
        return wins

    def attention_main():
        blocks = [visit(r, first_windows(r), True) for r in range(n_qblk)]
        n_stages = 3
        for step in range(n_qblk + (n_stages - 1) * ATTN_SKEW):
            for r in reversed(range(n_qblk)):
                stage, phase = divmod(step - r, ATTN_SKEW)
                if phase == 0 and 0 <= stage < n_stages and step - r >= 0:
                    next(blocks[r])
                    yield

    gate1 = mod_ref[0, 2:3, :]
    shift = mod_ref[0, 3:4, :]
    scale = mod_ref[0, 4:5, :]
    gate2 = mod_ref[0, 5:6, :]
    n_chunks = d_ff // FF_CHUNK

    def ffn_main():
        mixed = (jnp.dot(yp_ref[0], w_out_ref[0:d_pool, :], preferred_element_type=f32)
                 + jnp.dot(o_scr[...], w_out_ref[d_pool:, :], preferred_element_type=f32))
        x1 = x_ref[0] + gate1 * mixed
        inv_rms = lax.rsqrt(jnp.mean(x1 * x1, axis=-1, keepdims=True) + EPS)
        hb = ((x1 * inv_rms) * (g2_ref[...] * (1.0 + scale)) + shift).astype(bf16)
        yield

        def conv(c, cols):
            up = jnp.dot(hb, w_up_ref[:, cols], preferred_element_type=f32)
            ext = jnp.concatenate([halo_ref[c], up], axis=0)
            halo_ref[c] = up[tm - CONV_HALO:, :]
            prev1 = pltpu.roll(ext, 1, axis=0)[CONV_HALO:, :]
            prev2 = pltpu.roll(ext, 2, axis=0)[CONV_HALO:, :]
            return (cw_ref[2:3, cols] * up + cw_ref[1:2, cols] * prev1
                    + cw_ref[0:1, cols] * prev2 + cb_ref[:, cols])

        for c in range(n_chunks):
            gate = conv(c, slice(c * FF_CHUNK, (c + 1) * FF_CHUNK))
            val = conv(n_chunks + c, slice(d_ff + c * FF_CHUNK, d_ff + (c + 1) * FF_CHUNK))
            act = gate / (1.0 + jnp.exp2(gate * -LOG2_E)) * val
            act_ref[:, c * FF_CHUNK:(c + 1) * FF_CHUNK] = act.astype(bf16)
            yield
        ffn = jnp.dot(act_ref[...], w_down_ref[...], preferred_element_type=f32)
        out_ref[0] = x1 + gate2 * ffn
        yield

    for _ in itertools.zip_longest(attention_main(), ffn_main()):
        pass

    def unfinished():
        return jnp.max(run_ref[...]) > LOG_WEIGHT_ZERO

    def drain(gen):
        for _ in gen:
            pass

    go = unfinished()

    @pl.when(go)
    def _():
        for r in range(1, n_qblk):
            wins = []
            seen = r * Q_BLOCK - K_WIDE
            if seen > 0:
                wins.append(_Window(kc_ref, vc_ref, 0, seen, False, False))
            top = K_WIDE + min(seen, 0)
            wins.append(_Window(kp_ref, vp_ref, 0, top, False, True))
            drain(visit(r, wins, False))

    def cond(state):
        jb, more = state
        return jnp.logical_and(jb >= 0, more)

    def body(state):
        jb, _ = state
        src = pl.ds(pl.multiple_of(jb * Q_BLOCK, Q_BLOCK), Q_BLOCK)
        copies = [pltpu.make_async_copy(k_hbm.at[batch_a, src, :], kbuf, sem.at[0]),
                  pltpu.make_async_copy(v_hbm.at[batch_a, src, :], vbuf, sem.at[1])]
        for cp in copies:
            cp.start()
        for cp in copies:
            cp.wait()
        for r in range(n_qblk):
            drain(visit(r, [_Window(kbuf, vbuf, 0, Q_BLOCK, False, False)], False))
        return jb - 1, unfinished()

    first = ta_in_batch * n_qblk - (K_WIDE // Q_BLOCK + 1)
    lax.while_loop(cond, body, (first, go))

    @pl.when(go)
    def _():
        for r in range(n_qblk):
            for c in range(n_cols):
                normalise(r, c, acc_ref[r * n_cols + c])


def _attn_ffn(x, yp, q, k, v, mod, out_g, g2, w_out, w_up, conv_w, conv_b, w_down, tm=512):
    bsz, s, d = x.shape
    d_pool = yp.shape[-1]
    d_attn = q.shape[-1]
    d_ff = w_down.shape[0]
    tpb = s // tm
    n_tiles = bsz * tpb
    n_slots = (tm // Q_BLOCK) * (d_attn // LANES)
    idx = lax.broadcasted_iota(jnp.int32, (K_WIDE, K_WIDE), 0)
    from_here = (idx >= idx.T).astype(bf16)

    def attn_tile(step):
        t = jnp.minimum(step, n_tiles - 1)
        return t // tpb, t % tpb

    def ffn_tile(step):
        t = jnp.maximum(step - 1, 0)
        return t // tpb, t % tpb

    def a_spec(width):
        return pl.BlockSpec((1, tm, width), lambda i: (*attn_tile(i), 0))

    def prev_spec(width):
        per_tile = tm // K_WIDE
        return pl.BlockSpec((1, K_WIDE, width),
                            lambda i: (attn_tile(i)[0], jnp.maximum(attn_tile(i)[1] * per_tile - 1, 0), 0))

    def f_spec(width):
        return pl.BlockSpec((1, tm, width), lambda i: (*ffn_tile(i), 0))

    hbm = pl.BlockSpec(memory_space=pl.ANY)
    return pl.pallas_call(
        functools.partial(_attn_ffn_kernel, tiles_per_batch=tpb, d_pool=d_pool, d_ff=d_ff),
        grid=(n_tiles + 1,),
        in_specs=[
            a_spec(d_attn), a_spec(d_attn), a_spec(d_attn), prev_spec(d_attn), prev_spec(d_attn), hbm, hbm,
            _const_spec(out_g.shape), _const_spec(from_here.shape),
            f_spec(d), f_spec(d_pool),
            pl.BlockSpec((1,) + mod.shape[1:], lambda i: (ffn_tile(i)[0], 0, 0)),
            _const_spec(g2.shape),
            _const_spec(w_out.shape),
            _const_spec(w_up.shape),
            _const_spec(conv_w.shape),
            _const_spec(conv_b.shape),
            _const_spec(w_down.shape),
        ],
        out_specs=f_spec(d),
        out_shape=jax.ShapeDtypeStruct((bsz, s, d), x.dtype),
        scratch_shapes=[
            pltpu.VMEM((tm, d_attn), bf16),
            pltpu.VMEM((2 * d_ff // FF_CHUNK, CONV_HALO, FF_CHUNK), f32),
            pltpu.VMEM((tm, d_ff), bf16),
            pltpu.VMEM((n_slots, 2 * Q_BLOCK, 1), f32),
            pltpu.VMEM((n_slots, 2 * Q_BLOCK, LANES), f32),
            pltpu.VMEM((Q_BLOCK, d_attn), bf16),
            pltpu.VMEM((Q_BLOCK, d_attn), bf16),
            pltpu.SemaphoreType.DMA((2,)),
        ],
        name="attn_convffn",
        compiler_params=pltpu.CompilerParams(
            dimension_semantics=("arbitrary",), vmem_limit_bytes=FUSED_VMEM_LIMIT),
    )(q, k, v, k, v, k, v, out_g, from_here, x, yp, mod, g2, w_out, w_up, conv_w, conv_b, w_down)


def kernel(x, c, ada_w, ada_b, norm1_g, w_in, pool_w, pool_b, pool_scale, q_norm_g, k_norm_g, attn_out_g,
           w_out, norm2_g, w_up, conv_w, conv_b, w_down):
    depth = ada_w.shape[0]
    d = x.shape[-1]
    row = lambda a: a.reshape(1, -1)
    for l in range(depth):
        mod, w_in_b, pool_w_b = _adaln(c, ada_w[l], ada_b[l], (w_in[l], pool_w[l].reshape(-1, pool_w.shape[-1])))
        mod = mod.reshape(c.shape[0], -1, d)
        yp, q, k, v, w_out_b, w_up_b, w_down_b = _inproj(
            x, mod, row(norm1_g[l]), w_in_b, pool_w_b.reshape(pool_w[l].shape), row(pool_b[l]),
            row(pool_scale[l]), row(q_norm_g[l]), row(k_norm_g[l]),
            (w_out[l], w_up[l], w_down[l]))
        x = _attn_ffn(x, yp, q, k, v, mod, row(attn_out_g[l]), row(norm2_g[l]), w_out_b, w_up_b,
                      conv_w[l], row(conv_b[l]), w_down_b)
    return x
```

```python
import collections
import functools
import itertools
import math

import jax
import jax.numpy as jnp
from jax import lax
from jax.experimental import pallas as pl
from jax.experimental.pallas import tpu as pltpu

EPS = 1e-6
HEAD_DIM = 64
N_POOL_GROUPS = 4
POOL_WINDOWS = (2, 4, 8, 16)
POOL_HALO = 16
CONV_WIDTH = 3
CONV_HALO = 8
LANES = 128
BF16_SUBLANES = 16
Q_BLOCK = 128
K_WIDE = 256
FF_CHUNK = 256
ATTN_SKEW = 2
LOG_WEIGHT_ZERO = -104.7
LOG2_E = math.log2(math.e)
VMEM_LIMIT = 48 * 1024 * 1024
FUSED_VMEM_LIMIT = 62 * 1024 * 1024

bf16 = jnp.bfloat16
f32 = jnp.float32

_Window = collections.namedtuple("_Window", "k v start width diag prev")


def _const_spec(shape):
    nd = len(shape)
    return pl.BlockSpec(shape, lambda *_: (0,) * nd, pipeline_mode=pl.Buffered(1))


def _adaln_kernel(ct_ref, w_ref, b_ref, *rest):
    n_cast = (len(rest) - 1) // 2
    cast_src, out_ref, cast_dst = rest[:n_cast], rest[n_cast], rest[n_cast + 1:]

    @pl.when(pl.program_id(0) == 0)
    def _():
        out_ref[...] = jnp.broadcast_to(b_ref[...], out_ref.shape)

    ct = ct_ref[...]
    act = ct / (1.0 + jnp.exp(-ct))
    w = w_ref[...]
    rows = []
    for b in range(ct.shape[1]):
        rows.append(jnp.sum(act[:, b:b + 1] * w, axis=0, keepdims=True))
    out_ref[...] += jnp.concatenate(rows, axis=0)

    for src, dst in zip(cast_src, cast_dst):
        dst[...] = src[...].astype(dst.dtype)


def _adaln(c, ada_w, ada_b, later_weights, tk=256):
    bsz, d = c.shape
    n = ada_w.shape[1]
    steps = d // tk

    def row_slice(w):
        rows, cols = w.shape
        assert rows % (steps * BF16_SUBLANES) == 0, w.shape
        return pl.BlockSpec((rows // steps, cols), lambda k: (k, 0))

    cast_specs = [row_slice(w) for w in later_weights]
    return pl.pallas_call(
        _adaln_kernel,
        grid=(steps,),
        in_specs=[
            pl.BlockSpec((tk, bsz), lambda k: (k, 0)),
            pl.BlockSpec((tk, n), lambda k: (k, 0)),
            pl.BlockSpec((1, n), lambda k: (0, 0)),
        ] + cast_specs,
        out_specs=[pl.BlockSpec((bsz, n), lambda k: (0, 0))] + cast_specs,
        out_shape=[jax.ShapeDtypeStruct((bsz, n), f32)]
        + [jax.ShapeDtypeStruct(w.shape, bf16) for w in later_weights],
        name="adaln",
        compiler_params=pltpu.CompilerParams(dimension_semantics=("arbitrary",)),
    )(c.T, ada_w, ada_b.reshape(1, n), *later_weights)


def _inproj_kernel(x_ref, mod_ref, g1_ref, w_in_ref, pool_w_ref, pool_b_ref, pool_s_ref,
                   qg_ref, kg_ref, *rest, d_pool, d_attn):
    n_cast = (len(rest) - 5) // 2
    cast_src = rest[:n_cast]
    yp_ref, q_ref, k_ref, v_ref = rest[n_cast:n_cast + 4]
    cast_dst = rest[n_cast + 4:2 * n_cast + 4]
    halo_ref = rest[-1]
    j = pl.program_id(1)
    tm = x_ref.shape[1]

    @pl.when(j == 0)
    def _():
        halo_ref[...] = jnp.zeros_like(halo_ref)

    x = x_ref[0]
    shift = mod_ref[0, 0:1, :]
    scale = mod_ref[0, 1:2, :]
    inv_rms = lax.rsqrt(jnp.mean(x * x, axis=-1, keepdims=True) + EPS)
    h = (x * inv_rms) * (g1_ref[...] * (1.0 + scale)) + shift
    hb = h.astype(bf16)

    project = lambda lo, hi: jnp.dot(hb, w_in_ref[:, lo:hi], preferred_element_type=f32)
    u = project(0, d_pool)
    tq = project(d_pool, d_pool + d_attn)
    tk = project(d_pool + d_attn, d_pool + 2 * d_attn)
    v_ref[0] = project(d_pool + 2 * d_attn, d_pool + 3 * d_attn).astype(v_ref.dtype)

    ext = jnp.concatenate([halo_ref[...], u], axis=0)
    halo_ref[...] = u[tm - POOL_HALO:, :]
    t1 = (j * tm + 1 + lax.broadcasted_iota(jnp.int32, (tm, 1), 0))
    ys = []
    for g, w in enumerate(POOL_WINDOWS):
        cols = slice(g * LANES, (g + 1) * LANES)
        s = ext[:, cols]
        span = 1
        while span < w:
            s = s + pltpu.roll(s, span, axis=0)
            span *= 2
        count = jnp.minimum(t1, w).astype(f32)
        pooled = s[POOL_HALO:, :] / count - u[:, cols]
        y = jnp.dot(pooled.astype(bf16), pool_w_ref[g], preferred_element_type=f32)
        ys.append((y + pool_b_ref[:, cols]) * pool_s_ref[:, cols])
    yp_ref[0] = jnp.concatenate(ys, axis=-1).astype(yp_ref.dtype)

    head0 = lax.broadcasted_iota(jnp.int32, (1, LANES), 1) < HEAD_DIM

    def head_norm(t, g_ref, post):
        gain = jnp.concatenate([g_ref[...], g_ref[...]], axis=-1) * post
        cols = []
        for c in range(d_attn // LANES):
            tc = t[:, c * LANES:(c + 1) * LANES]
            sq = tc * tc
            ssq0 = jnp.sum(jnp.where(head0, sq, 0.0), axis=-1, keepdims=True)
            ssq1 = jnp.sum(jnp.where(head0, 0.0, sq), axis=-1, keepdims=True)
            inv = jnp.where(head0, lax.rsqrt(ssq0 * (1.0 / HEAD_DIM) + EPS),
                            lax.rsqrt(ssq1 * (1.0 / HEAD_DIM) + EPS))
            cols.append((tc * inv) * gain)
        return jnp.concatenate(cols, axis=-1)

    q_ref[0] = head_norm(tq, qg_ref, -1.0 / math.sqrt(HEAD_DIM)).astype(q_ref.dtype)
    k_ref[0] = head_norm(tk, kg_ref, 1.0).astype(k_ref.dtype)

    for src, dst in zip(cast_src, cast_dst):
        dst[...] = src[...].astype(dst.dtype)


def _inproj(x, mod, g1, w_in, pool_w, pool_b, pool_s, qg, kg, later_weights, tm=1024):
    bsz, s, d = x.shape
    d_pool = pool_b.shape[-1]
    d_attn = (w_in.shape[1] - d_pool) // 3
    n_j = s // tm
    tile = lambda width: pl.BlockSpec((1, tm, width), lambda b, j: (b, j, 0))
    out = jax.ShapeDtypeStruct((bsz, s, d_attn), bf16)

    def row_slice(w):
        rows, cols = w.shape
        n_blocks = bsz * n_j
        while rows % (n_blocks * BF16_SUBLANES):
            assert n_blocks % 2 == 0, w.shape
            n_blocks //= 2
        return pl.BlockSpec((rows // n_blocks, cols), lambda b, j: (jnp.minimum(b * n_j + j, n_blocks - 1), 0))

    cast_specs = [row_slice(w) for w in later_weights]
    return pl.pallas_call(
        functools.partial(_inproj_kernel, d_pool=d_pool, d_attn=d_attn),
        grid=(bsz, n_j),
        in_specs=[
            tile(d),
            pl.BlockSpec((1,) + mod.shape[1:], lambda b, j: (b, 0, 0)),
            _const_spec(g1.shape),
            _const_spec(w_in.shape),
            _const_spec(pool_w.shape),
            _const_spec(pool_b.shape),
            _const_spec(pool_s.shape),
            _const_spec(qg.shape),
            _const_spec(kg.shape),
        ] + cast_specs,
        out_specs=[tile(d_pool), tile(d_attn), tile(d_attn), tile(d_attn)] + cast_specs,
        out_shape=[jax.ShapeDtypeStruct((bsz, s, d_pool), bf16), out, out, out]
        + [jax.ShapeDtypeStruct(w.shape, bf16) for w in later_weights],
        scratch_shapes=[pltpu.VMEM((POOL_HALO, d_pool), f32)],
        name="inproj",
        compiler_params=pltpu.CompilerParams(
            dimension_semantics=("arbitrary", "arbitrary"), vmem_limit_bytes=VMEM_LIMIT),
    )(x, mod, g1, w_in, pool_w, pool_b, pool_s, qg, kg, *later_weights)


def _attn_ffn_kernel(q_ref, kc_ref, vc_ref, kp_ref, vp_ref, k_hbm, v_hbm, og_ref, from_ref,
                     x_ref, yp_ref, mod_ref, g2_ref, w_out_ref, w_up_ref, cw_ref, cb_ref, w_down_ref,
                     out_ref,
                     o_scr, halo_ref, act_ref, run_ref, acc_ref, kbuf, vbuf, sem,
                     *, tiles_per_batch, d_pool, d_ff):
    s = pl.program_id(0)
    n_tiles = pl.num_programs(0) - 1
    ta = jnp.minimum(s, n_tiles - 1)
    ta_in_batch = lax.rem(ta, tiles_per_batch)
    batch_a = lax.div(ta, tiles_per_batch)
    tf_in_batch = lax.rem(jnp.maximum(s - 1, 0), tiles_per_batch)
    tm = q_ref.shape[1]
    n_qblk = tm // Q_BLOCK
    n_cols = q_ref.shape[2] // LANES

    @pl.when(s == 0)
    def _():
        o_scr[...] = jnp.zeros_like(o_scr)

    @pl.when(tf_in_batch == 0)
    def _():
        halo_ref[...] = jnp.zeros_like(halo_ref)

    lane = lax.broadcasted_iota(jnp.int32, (1, LANES), 1)
    head0 = lane < HEAD_DIM
    prev_valid = ta_in_batch > 0

    def rows(ref, start, width, c):
        cols = slice(c * LANES, (c + 1) * LANES)
        if len(ref.shape) == 3:
            return ref[0, start:start + width, cols]
        return ref[start:start + width, cols]

    def stacked_q(r, c):
        q = q_ref[0, r * Q_BLOCK:(r + 1) * Q_BLOCK, c * LANES:(c + 1) * LANES]
        zq = jnp.zeros_like(q)
        return jnp.concatenate([jnp.where(head0, q, zq), jnp.where(head0, zq, q)], axis=0)

    def scores(c, q2, win):
        kb = rows(win.k, win.start, win.width, c)
        return lax.dot_general(q2, kb, (((1,), (1,)), ((), ())), preferred_element_type=f32)

    def log_terms(zn, win):
        width = zn.shape[1]
        log_1m = jnp.minimum(zn, 0.0) - jnp.log(1.0 + jnp.exp2(jnp.abs(zn) * -LOG2_E))
        mask = None
        if win.diag:
            row = lax.broadcasted_iota(jnp.int32, zn.shape, 0) & (Q_BLOCK - 1)
            mask = lax.broadcasted_iota(jnp.int32, zn.shape, 1) < row
        elif win.prev:
            mask = prev_valid
        if mask is not None:
            log_1m = jnp.where(mask, log_1m, 0.0)
        from_here = jnp.dot(log_1m.astype(bf16), from_ref[0:width, 0:width], preferred_element_type=f32)
        return from_here - zn, jnp.sum(log_1m, axis=1, keepdims=True), mask

    def weighted_values(c, win, log_w, mask, run):
        a = jnp.exp(log_w if run is None else log_w + run)
        if mask is not None:
            a = jnp.where(mask, a, 0.0)
        return jnp.dot(a.astype(bf16), rows(win.v, win.start, win.width, c), preferred_element_type=f32)

    def visit(r, windows, fresh):
        q2 = [stacked_q(r, c) for c in range(n_cols)]
        zs = [[scores(c, q2[c], w) for w in windows] for c in range(n_cols)]
        yield
        terms = [[log_terms(z, w) for z, w in zip(zs[c], windows)] for c in range(n_cols)]
        yield
        for c in range(n_cols):
            slot = r * n_cols + c
            run, acc = (None, None) if fresh else (run_ref[slot], acc_ref[slot])
            for w, (log_w, row_sum, mask) in zip(windows, terms[c]):
                pv = weighted_values(c, w, log_w, mask, run)
                acc = pv if acc is None else acc + pv
                run = row_sum if run is None else run + row_sum
            run_ref[slot] = run
            acc_ref[slot] = acc
            if fresh:
                normalise(r, c, acc)
        yield

    def normalise(r, c, acc):
        cols = slice(c * LANES, (c + 1) * LANES)
        o = jnp.where(head0, acc[0:Q_BLOCK, :], acc[Q_BLOCK:, :])
        o2 = o * o
        ssq0 = jnp.sum(jnp.where(head0, o2, 0.0), axis=-1, keepdims=True)
        ssq1 = jnp.sum(jnp.where(head0, 0.0, o2), axis=-1, keepdims=True)
        ms = jnp.where(head0, ssq0, ssq1) * (1.0 / HEAD_DIM)
        gain = jnp.concatenate([og_ref[2 * c:2 * c + 1, :], og_ref[2 * c + 1:2 * c + 2, :]], axis=-1)
        o_scr[r * Q_BLOCK:(r + 1) * Q_BLOCK, cols] = (o * lax.rsqrt(ms + EPS) * gain).astype(o_scr.dtype)

    def first_windows(r):
        wins = [_Window(kc_ref, vc_ref, r * Q_BLOCK, Q_BLOCK, True, False)]
        start = r * Q_BLOCK - K_WIDE
        if start >= 0:
            wins.append(_Window(kc_ref, vc_ref, start, K_WIDE, False, False))
        else:
            if r > 0:
                wins.append(_Window(kc_ref, vc_ref, 0, r * Q_BLOCK, False, False))
            wins.append(_Window(kp_ref, vp_ref, K_WIDE + start, -start, False, True))
        return wins

    def attention_main():
        blocks = [visit(r, first_windows(r), True) for r in range(n_qblk)]
        n_stages = 3
        for step in range(n_qblk + (n_stages - 1) * ATTN_SKEW):
            for r in reversed(range(n_qblk)):
                stage, phase = divmod(step - r, ATTN_SKEW)
                if phase == 0 and 0 <= stage < n_stages and step - r >= 0:
                    next(blocks[r])
                    yield

    gate1 = mod_ref[0, 2:3, :]
    shift = mod_ref[0, 3:4, :]
    scale = mod_ref[0, 4:5, :]
    gate2 = mod_ref[0, 5:6, :]
    n_chunks = d_ff // FF_CHUNK

    def ffn_main():
        mixed = (jnp.dot(yp_ref[0], w_out_ref[0:d_pool, :], preferred_element_type=f32)
                 + jnp.dot(o_scr[...], w_out_ref[d_pool:, :], preferred_element_type=f32))
        x1 = x_ref[0] + gate1 * mixed
        inv_rms = lax.rsqrt(jnp.mean(x1 * x1, axis=-1, keepdims=True) + EPS)
        hb = ((x1 * inv_rms) * (g2_ref[...] * (1.0 + scale)) + shift).astype(bf16)
        yield

        def conv(c, cols):
            up = jnp.dot(hb, w_up_ref[:, cols], preferred_element_type=f32)
            ext = jnp.concatenate([halo_ref[c], up], axis=0)
            halo_ref[c] = up[tm - CONV_HALO:, :]
            prev1 = pltpu.roll(ext, 1, axis=0)[CONV_HALO:, :]
            prev2 = pltpu.roll(ext, 2, axis=0)[CONV_HALO:, :]
            return (cw_ref[2:3, cols] * up + cw_ref[1:2, cols] * prev1
                    + cw_ref[0:1, cols] * prev2 + cb_ref[:, cols])

        for c in range(n_chunks):
            gate = conv(c, slice(c * FF_CHUNK, (c + 1) * FF_CHUNK))
            val = conv(n_chunks + c, slice(d_ff + c * FF_CHUNK, d_ff + (c + 1) * FF_CHUNK))
            act = gate / (1.0 + jnp.exp2(gate * -LOG2_E)) * val
            act_ref[:, c * FF_CHUNK:(c + 1) * FF_CHUNK] = act.astype(bf16)
            yield
        ffn = jnp.dot(act_ref[...], w_down_ref[...], preferred_element_type=f32)
        out_ref[0] = x1 + gate2 * ffn
        yield

    for _ in itertools.zip_longest(attention_main(), ffn_main()):
        pass

    def unfinished():
        return jnp.max(run_ref[...]) > LOG_WEIGHT_ZERO

    def drain(gen):
        for _ in gen:
            pass

    go = unfinished()

    @pl.when(go)
    def _():
        for r in range(1, n_qblk):
            wins = []
            seen = r * Q_BLOCK - K_WIDE
            if seen > 0:
                wins.append(_Window(kc_ref, vc_ref, 0, seen, False, False))
            top = K_WIDE + min(seen, 0)
            wins.append(_Window(kp_ref, vp_ref, 0, top, False, True))
            drain(visit(r, wins, False))

    def cond(state):
        jb, more = state
        return jnp.logical_and(jb >= 0, more)

    def body(state):
        jb, _ = state
        src = pl.ds(pl.multiple_of(jb * Q_BLOCK, Q_BLOCK), Q_BLOCK)
        copies = [pltpu.make_async_copy(k_hbm.at[batch_a, src, :], kbuf, sem.at[0]),
                  pltpu.make_async_copy(v_hbm.at[batch_a, src, :], vbuf, sem.at[1])]
        for cp in copies:
            cp.start()
        for cp in copies:
            cp.wait()
        for r in range(n_qblk):
            drain(visit(r, [_Window(kbuf, vbuf, 0, Q_BLOCK, False, False)], False))
        return jb - 1, unfinished()

    first = ta_in_batch * n_qblk - (K_WIDE // Q_BLOCK + 1)
    lax.while_loop(cond, body, (first, go))

    @pl.when(go)
    def _():
        for r in range(n_qblk):
            for c in range(n_cols):
                normalise(r, c, acc_ref[r * n_cols + c])


def _attn_ffn(x, yp, q, k, v, mod, out_g, g2, w_out, w_up, conv_w, conv_b, w_down, tm=512):
    bsz, s, d = x.shape
    d_pool = yp.shape[-1]
    d_attn = q.shape[-1]
    d_ff = w_down.shape[0]
    tpb = s // tm
    n_tiles = bsz * tpb
    n_slots = (tm // Q_BLOCK) * (d_attn // LANES)
    idx = lax.broadcasted_iota(jnp.int32, (K_WIDE, K_WIDE), 0)
    from_here = (idx >= idx.T).astype(bf16)

    def attn_tile(step):
        t = jnp.minimum(step, n_tiles - 1)
        return t // tpb, t % tpb

    def ffn_tile(step):
        t = jnp.maximum(step - 1, 0)
        return t // tpb, t % tpb

    def a_spec(width):
        return pl.BlockSpec((1, tm, width), lambda i: (*attn_tile(i), 0))

    def prev_spec(width):
        per_tile = tm // K_WIDE
        return pl.BlockSpec((1, K_WIDE, width),
                            lambda i: (attn_tile(i)[0], jnp.maximum(attn_tile(i)[1] * per_tile - 1, 0), 0))

    def f_spec(width):
        return pl.BlockSpec((1, tm, width), lambda i: (*ffn_tile(i), 0))

    hbm = pl.BlockSpec(memory_space=pl.ANY)
    return pl.pallas_call(
        functools.partial(_attn_ffn_kernel, tiles_per_batch=tpb, d_pool=d_pool, d_ff=d_ff),
        grid=(n_tiles + 1,),
        in_specs=[
            a_spec(d_attn), a_spec(d_attn), a_spec(d_attn), prev_spec(d_attn), prev_spec(d_attn), hbm, hbm,
            _const_spec(out_g.shape), _const_spec(from_here.shape),
            f_spec(d), f_spec(d_pool),
            pl.BlockSpec((1,) + mod.shape[1:], lambda i: (ffn_tile(i)[0], 0, 0)),
            _const_spec(g2.shape),
            _const_spec(w_out.shape),
            _const_spec(w_up.shape),
            _const_spec(conv_w.shape),
            _const_spec(conv_b.shape),
            _const_spec(w_down.shape),
        ],
        out_specs=f_spec(d),
        out_shape=jax.ShapeDtypeStruct((bsz, s, d), x.dtype),
        scratch_shapes=[
            pltpu.VMEM((tm, d_attn), bf16),
            pltpu.VMEM((2 * d_ff // FF_CHUNK, CONV_HALO, FF_CHUNK), f32),
            pltpu.VMEM((tm, d_ff), bf16),
            pltpu.VMEM((n_slots, 2 * Q_BLOCK, 1), f32),
            pltpu.VMEM((n_slots, 2 * Q_BLOCK, LANES), f32),
            pltpu.VMEM((Q_BLOCK, d_attn), bf16),
            pltpu.VMEM((Q_BLOCK, d_attn), bf16),
            pltpu.SemaphoreType.DMA((2,)),
        ],
        name="attn_convffn",
        compiler_params=pltpu.CompilerParams(
            dimension_semantics=("arbitrary",), vmem_limit_bytes=FUSED_VMEM_LIMIT),
    )(q, k, v, k, v, k, v, out_g, from_here, x, yp, mod, g2, w_out, w_up, conv_w, conv_b, w_down)


def kernel(x, c, ada_w, ada_b, norm1_g, w_in, pool_w, pool_b, pool_scale, q_norm_g, k_norm_g, attn_out_g,
           w_out, norm2_g, w_up, conv_w, conv_b, w_down):
    depth = ada_w.shape[0]
    d = x.shape[-1]
    row = lambda a: a.reshape(1, -1)
    for l in range(depth):
        mod, w_in_b, pool_w_b = _adaln(c, ada_w[l], ada_b[l], (w_in[l], pool_w[l].reshape(-1, pool_w.shape[-1])))
        mod = mod.reshape(c.shape[0], -1, d)
        yp, q, k, v, w_out_b, w_up_b, w_down_b = _inproj(
            x, mod, row(norm1_g[l]), w_in_b, pool_w_b.reshape(pool_w[l].shape), row(pool_b[l]),
            row(pool_scale[l]), row(q_norm_g[l]), row(k_norm_g[l]),
            (w_out[l], w_up[l], w_down[l]))
        x = _attn_ffn(x, yp, q, k, v, mod, attn_out_g[l], row(norm2_g[l]), w_out_b, w_up_b,
                      conv_w[l], row(conv_b[l]), w_down_b)
    return x
```

```python
import collections
import functools
import itertools
import math

import jax
import jax.numpy as jnp
from jax import lax
from jax.experimental import pallas as pl
from jax.experimental.pallas import tpu as pltpu

EPS = 1e-6
HEAD_DIM = 64
POOL_WINDOWS = (2, 4, 8, 16)
POOL_HALO = 16
CONV_HALO = 8
LANES = 128
BF16_SUBLANES = 16
Q_BLOCK = 128
K_WIDE = 256
FF_CHUNK = 256
ATTN_SKEW = 2
LOG_WEIGHT_ZERO = -104.7
LOG2_E = math.log2(math.e)
VMEM_LIMIT = 48 * 1024 * 1024
FUSED_VMEM_LIMIT = 62 * 1024 * 1024

bf16 = jnp.bfloat16
f32 = jnp.float32

_Window = collections.namedtuple("_Window", "k v start width diag prev")


def _const_spec(shape):
    nd = len(shape)
    return pl.BlockSpec(shape, lambda *_: (0,) * nd, pipeline_mode=pl.Buffered(1))


def _adaln_kernel(ct_ref, w_ref, b_ref, *rest):
    n_cast = (len(rest) - 1) // 2
    cast_src, out_ref, cast_dst = rest[:n_cast], rest[n_cast], rest[n_cast + 1:]

    @pl.when(pl.program_id(0) == 0)
    def _():
        out_ref[...] = jnp.broadcast_to(b_ref[...], out_ref.shape)

    ct = ct_ref[...]
    act = ct / (1.0 + jnp.exp(-ct))
    w = w_ref[...]
    rows = []
    for b in range(ct.shape[1]):
        rows.append(jnp.sum(act[:, b:b + 1] * w, axis=0, keepdims=True))
    out_ref[...] += jnp.concatenate(rows, axis=0)

    for src, dst in zip(cast_src, cast_dst):
        dst[...] = src[...].astype(dst.dtype)


def _adaln(c, ada_w, ada_b, later_weights, tk=256):
    bsz, d = c.shape
    n = ada_w.shape[1]
    steps = d // tk

    def row_slice(w):
        rows, cols = w.shape
        assert rows % (steps * BF16_SUBLANES) == 0, w.shape
        return pl.BlockSpec((rows // steps, cols), lambda k: (k, 0))

    cast_specs = [row_slice(w) for w in later_weights]
    return pl.pallas_call(
        _adaln_kernel,
        grid=(steps,),
        in_specs=[
            pl.BlockSpec((tk, bsz), lambda k: (k, 0)),
            pl.BlockSpec((tk, n), lambda k: (k, 0)),
            pl.BlockSpec((1, n), lambda k: (0, 0)),
        ] + cast_specs,
        out_specs=[pl.BlockSpec((bsz, n), lambda k: (0, 0))] + cast_specs,
        out_shape=[jax.ShapeDtypeStruct((bsz, n), f32)]
        + [jax.ShapeDtypeStruct(w.shape, bf16) for w in later_weights],
        name="adaln",
        compiler_params=pltpu.CompilerParams(dimension_semantics=("arbitrary",)),
    )(c.T, ada_w, ada_b.reshape(1, n), *later_weights)


def _inproj_kernel(x_ref, mod_ref, g1_ref, w_in_ref, pool_w_ref, pool_b_ref, pool_s_ref,
                   qg_ref, kg_ref, *rest, d_pool, d_attn):
    n_cast = (len(rest) - 5) // 2
    cast_src = rest[:n_cast]
    yp_ref, q_ref, k_ref, v_ref = rest[n_cast:n_cast + 4]
    cast_dst = rest[n_cast + 4:2 * n_cast + 4]
    halo_ref = rest[-1]
    j = pl.program_id(1)
    tm = x_ref.shape[1]

    @pl.when(j == 0)
    def _():
        halo_ref[...] = jnp.zeros_like(halo_ref)

    x = x_ref[0]
    shift = mod_ref[0, 0:1, :]
    scale = mod_ref[0, 1:2, :]
    inv_rms = lax.rsqrt(jnp.mean(x * x, axis=-1, keepdims=True) + EPS)
    h = (x * inv_rms) * (g1_ref[...] * (1.0 + scale)) + shift
    hb = h.astype(bf16)

    project = lambda lo, hi: jnp.dot(hb, w_in_ref[:, lo:hi], preferred_element_type=f32)
    u = project(0, d_pool)
    tq = project(d_pool, d_pool + d_attn)
    tk = project(d_pool + d_attn, d_pool + 2 * d_attn)
    v_ref[0] = project(d_pool + 2 * d_attn, d_pool + 3 * d_attn).astype(v_ref.dtype)

    ext = jnp.concatenate([halo_ref[...], u], axis=0)
    halo_ref[...] = u[tm - POOL_HALO:, :]
    t1 = (j * tm + 1 + lax.broadcasted_iota(jnp.int32, (tm, 1), 0))
    ys = []
    for g, w in enumerate(POOL_WINDOWS):
        cols = slice(g * LANES, (g + 1) * LANES)
        s = ext[:, cols]
        span = 1
        while span < w:
            s = s + pltpu.roll(s, span, axis=0)
            span *= 2
        count = jnp.minimum(t1, w).astype(f32)
        pooled = s[POOL_HALO:, :] / count - u[:, cols]
        y = jnp.dot(pooled.astype(bf16), pool_w_ref[g], preferred_element_type=f32)
        ys.append((y + pool_b_ref[:, cols]) * pool_s_ref[:, cols])
    yp_ref[0] = jnp.concatenate(ys, axis=-1).astype(yp_ref.dtype)

    head0 = lax.broadcasted_iota(jnp.int32, (1, LANES), 1) < HEAD_DIM

    def head_norm(t, g_ref, post):
        gain = jnp.concatenate([g_ref[...], g_ref[...]], axis=-1) * post
        cols = []
        for c in range(d_attn // LANES):
            tc = t[:, c * LANES:(c + 1) * LANES]
            sq = tc * tc
            ssq0 = jnp.sum(jnp.where(head0, sq, 0.0), axis=-1, keepdims=True)
            ssq1 = jnp.sum(jnp.where(head0, 0.0, sq), axis=-1, keepdims=True)
            inv = jnp.where(head0, lax.rsqrt(ssq0 * (1.0 / HEAD_DIM) + EPS),
                            lax.rsqrt(ssq1 * (1.0 / HEAD_DIM) + EPS))
            cols.append((tc * inv) * gain)
        return jnp.concatenate(cols, axis=-1)

    q_ref[0] = head_norm(tq, qg_ref, -1.0 / math.sqrt(HEAD_DIM)).astype(q_ref.dtype)
    k_ref[0] = head_norm(tk, kg_ref, 1.0).astype(k_ref.dtype)

    for src, dst in zip(cast_src, cast_dst):
        dst[...] = src[...].astype(dst.dtype)


def _inproj(x, mod, g1, w_in, pool_w, pool_b, pool_s, qg, kg, later_weights, tm=1024):
    bsz, s, d = x.shape
    d_pool = pool_b.shape[-1]
    d_attn = (w_in.shape[1] - d_pool) // 3
    n_j = s // tm
    tile = lambda width: pl.BlockSpec((1, tm, width), lambda b, j: (b, j, 0))
    out = jax.ShapeDtypeStruct((bsz, s, d_attn), bf16)

    def row_slice(w):
        rows, cols = w.shape
        n_blocks = bsz * n_j
        while rows % (n_blocks * BF16_SUBLANES):
            assert n_blocks % 2 == 0, w.shape
            n_blocks //= 2
        return pl.BlockSpec((rows // n_blocks, cols), lambda b, j: (jnp.minimum(b * n_j + j, n_blocks - 1), 0))

    cast_specs = [row_slice(w) for w in later_weights]
    return pl.pallas_call(
        functools.partial(_inproj_kernel, d_pool=d_pool, d_attn=d_attn),
        grid=(bsz, n_j),
        in_specs=[
            tile(d),
            pl.BlockSpec((1,) + mod.shape[1:], lambda b, j: (b, 0, 0)),
            _const_spec(g1.shape),
            _const_spec(w_in.shape),
            _const_spec(pool_w.shape),
            _const_spec(pool_b.shape),
            _const_spec(pool_s.shape),
            _const_spec(qg.shape),
            _const_spec(kg.shape),
        ] + cast_specs,
        out_specs=[tile(d_pool), tile(d_attn), tile(d_attn), tile(d_attn)] + cast_specs,
        out_shape=[jax.ShapeDtypeStruct((bsz, s, d_pool), bf16), out, out, out]
        + [jax.ShapeDtypeStruct(w.shape, bf16) for w in later_weights],
        scratch_shapes=[pltpu.VMEM((POOL_HALO, d_pool), f32)],
        name="inproj",
        compiler_params=pltpu.CompilerParams(
            dimension_semantics=("arbitrary", "arbitrary"), vmem_limit_bytes=VMEM_LIMIT),
    )(x, mod, g1, w_in, pool_w, pool_b, pool_s, qg, kg, *later_weights)


def _attn_ffn_kernel(q_ref, kc_ref, vc_ref, kp_ref, vp_ref, k_hbm, v_hbm, og_ref, from_ref,
                     x_ref, yp_ref, mod_ref, g2_ref, w_out_ref, w_up_ref, cw_ref, cb_ref, w_down_ref,
                     out_ref,
                     o_scr, halo_ref, act_ref, run_ref, acc_ref, kbuf, vbuf, sem,
                     *, tiles_per_batch, d_pool, d_ff):
    s = pl.program_id(0)
    n_tiles = pl.num_programs(0) - 1
    ta = jnp.minimum(s, n_tiles - 1)
    ta_in_batch = lax.rem(ta, tiles_per_batch)
    batch_a = lax.div(ta, tiles_per_batch)
    tf_in_batch = lax.rem(jnp.maximum(s - 1, 0), tiles_per_batch)
    tm = q_ref.shape[1]
    n_qblk = tm // Q_BLOCK
    n_cols = q_ref.shape[2] // LANES

    @pl.when(s == 0)
    def _():
        o_scr[...] = jnp.zeros_like(o_scr)

    @pl.when(tf_in_batch == 0)
    def _():
        halo_ref[...] = jnp.zeros_like(halo_ref)

    lane = lax.broadcasted_iota(jnp.int32, (1, LANES), 1)
    head0 = lane < HEAD_DIM
    prev_valid = ta_in_batch > 0

    def rows(ref, start, width, c):
        cols = slice(c * LANES, (c + 1) * LANES)
        if len(ref.shape) == 3:
            return ref[0, start:start + width, cols]
        return ref[start:start + width, cols]

    def stacked_q(r, c):
        q = q_ref[0, r * Q_BLOCK:(r + 1) * Q_BLOCK, c * LANES:(c + 1) * LANES]
        zq = jnp.zeros_like(q)
        return jnp.concatenate([jnp.where(head0, q, zq), jnp.where(head0, zq, q)], axis=0)

    def scores(c, q2, win):
        kb = rows(win.k, win.start, win.width, c)
        return lax.dot_general(q2, kb, (((1,), (1,)), ((), ())), preferred_element_type=f32)

    def log_terms(zn, win):
        width = zn.shape[1]
        log_1m = jnp.minimum(zn, 0.0) - jnp.log(1.0 + jnp.exp2(jnp.abs(zn) * -LOG2_E))
        mask = None
        if win.diag:
            row = lax.broadcasted_iota(jnp.int32, zn.shape, 0) & (Q_BLOCK - 1)
            mask = lax.broadcasted_iota(jnp.int32, zn.shape, 1) < row
        elif win.prev:
            mask = prev_valid
        if mask is not None:
            log_1m = jnp.where(mask, log_1m, 0.0)
        from_here = jnp.dot(log_1m.astype(bf16), from_ref[0:width, 0:width], preferred_element_type=f32)
        return from_here - zn, jnp.sum(log_1m, axis=1, keepdims=True), mask

    def weighted_values(c, win, log_w, mask, run):
        a = jnp.exp(log_w if run is None else log_w + run)
        if mask is not None:
            a = jnp.where(mask, a, 0.0)
        return jnp.dot(a.astype(bf16), rows(win.v, win.start, win.width, c), preferred_element_type=f32)

    def visit(r, windows, fresh):
        q2 = [stacked_q(r, c) for c in range(n_cols)]
        zs = [[scores(c, q2[c], w) for w in windows] for c in range(n_cols)]
        yield
        terms = [[log_terms(z, w) for z, w in zip(zs[c], windows)] for c in range(n_cols)]
        yield
        for c in range(n_cols):
            slot = r * n_cols + c
            run, acc = (None, None) if fresh else (run_ref[slot], acc_ref[slot])
            for w, (log_w, row_sum, mask) in zip(windows, terms[c]):
                pv = weighted_values(c, w, log_w, mask, run)
                acc = pv if acc is None else acc + pv
                run = row_sum if run is None else run + row_sum
            run_ref[slot] = run
            acc_ref[slot] = acc
            if fresh:
                normalise(r, c, acc)
        yield

    def normalise(r, c, acc):
        cols = slice(c * LANES, (c + 1) * LANES)
        o = jnp.where(head0, acc[0:Q_BLOCK, :], acc[Q_BLOCK:, :])
        o2 = o * o
        ssq0 = jnp.sum(jnp.where(head0, o2, 0.0), axis=-1, keepdims=True)
        ssq1 = jnp.sum(jnp.where(head0, 0.0, o2), axis=-1, keepdims=True)
        ms = jnp.where(head0, ssq0, ssq1) * (1.0 / HEAD_DIM)
        gain = jnp.concatenate([og_ref[2 * c:2 * c + 1, :], og_ref[2 * c + 1:2 * c + 2, :]], axis=-1)
        o_scr[r * Q_BLOCK:(r + 1) * Q_BLOCK, cols] = (o * lax.rsqrt(ms + EPS) * gain).astype(o_scr.dtype)

    def first_windows(r):
        wins = [_Window(kc_ref, vc_ref, r * Q_BLOCK, Q_BLOCK, True, False)]
        start = r * Q_BLOCK - K_WIDE
        if start >= 0:
            wins.append(_Window(kc_ref, vc_ref, start, K_WIDE, False, False))
        else:
            if r > 0:
                wins.append(_Window(kc_ref, vc_ref, 0, r * Q_BLOCK, False, False))
            wins.append(_Window(kp_ref, vp_ref, K_WIDE + start, -start, False, True))
        return wins

    def attention_main():
        blocks = [visit(r, first_windows(r), True) for r in range(n_qblk)]
        n_stages = 3
        for step in range(n_qblk + (n_stages - 1) * ATTN_SKEW):
            for r in reversed(range(n_qblk)):
                stage, phase = divmod(step - r, ATTN_SKEW)
                if phase == 0 and 0 <= stage < n_stages and step - r >= 0:
                    next(blocks[r])
                    yield

    gate1 = mod_ref[0, 2:3, :]
    shift = mod_ref[0, 3:4, :]
    scale = mod_ref[0, 4:5, :]
    gate2 = mod_ref[0, 5:6, :]
    n_chunks = d_ff // FF_CHUNK

    def ffn_main():
        mixed = (jnp.dot(yp_ref[0], w_out_ref[0:d_pool, :], preferred_element_type=f32)
                 + jnp.dot(o_scr[...], w_out_ref[d_pool:, :], preferred_element_type=f32))
        x1 = x_ref[0] + gate1 * mixed
        inv_rms = lax.rsqrt(jnp.mean(x1 * x1, axis=-1, keepdims=True) + EPS)
        hb = ((x1 * inv_rms) * (g2_ref[...] * (1.0 + scale)) + shift).astype(bf16)
        yield

        def conv(c, cols):
            up = jnp.dot(hb, w_up_ref[:, cols], preferred_element_type=f32)
            ext = jnp.concatenate([halo_ref[c], up], axis=0)
            halo_ref[c] = up[tm - CONV_HALO:, :]
            prev1 = pltpu.roll(ext, 1, axis=0)[CONV_HALO:, :]
            prev2 = pltpu.roll(ext, 2, axis=0)[CONV_HALO:, :]
            return (cw_ref[2:3, cols] * up + cw_ref[1:2, cols] * prev1
                    + cw_ref[0:1, cols] * prev2 + cb_ref[:, cols])

        for c in range(n_chunks):
            gate = conv(c, slice(c * FF_CHUNK, (c + 1) * FF_CHUNK))
            val = conv(n_chunks + c, slice(d_ff + c * FF_CHUNK, d_ff + (c + 1) * FF_CHUNK))
            act = gate / (1.0 + jnp.exp2(gate * -LOG2_E)) * val
            act_ref[:, c * FF_CHUNK:(c + 1) * FF_CHUNK] = act.astype(bf16)
            yield
        ffn = jnp.dot(act_ref[...], w_down_ref[...], preferred_element_type=f32)
        out_ref[0] = x1 + gate2 * ffn
        yield

    for _ in itertools.zip_longest(attention_main(), ffn_main()):
        pass

    def unfinished():
        return jnp.max(run_ref[...]) > LOG_WEIGHT_ZERO

    def drain(gen):
        for _ in gen:
            pass

    go = unfinished()

    @pl.when(go)
    def _():
        for r in range(1, n_qblk):
            wins = []
            seen = r * Q_BLOCK - K_WIDE
            if seen > 0:
                wins.append(_Window(kc_ref, vc_ref, 0, seen, False, False))
            top = K_WIDE + min(seen, 0)
            wins.append(_Window(kp_ref, vp_ref, 0, top, False, True))
            drain(visit(r, wins, False))

    def cond(state):
        jb, more = state
        return jnp.logical_and(jb >= 0, more)

    def body(state):
        jb, _ = state
        src = pl.ds(pl.multiple_of(jb * Q_BLOCK, Q_BLOCK), Q_BLOCK)
        copies = [pltpu.make_async_copy(k_hbm.at[batch_a, src, :], kbuf, sem.at[0]),
                  pltpu.make_async_copy(v_hbm.at[batch_a, src, :], vbuf, sem.at[1])]
        for cp in copies:
            cp.start()
        for cp in copies:
            cp.wait()
        for r in range(n_qblk):
            drain(visit(r, [_Window(kbuf, vbuf, 0, Q_BLOCK, False, False)], False))
        return jb - 1, unfinished()

    first = ta_in_batch * n_qblk - (K_WIDE // Q_BLOCK + 1)
    lax.while_loop(cond, body, (first, go))

    @pl.when(go)
    def _():
        for r in range(n_qblk):
            for c in range(n_cols):
                normalise(r, c, acc_ref[r * n_cols + c])


def _attn_ffn(x, yp, q, k, v, mod, out_g, g2, w_out, w_up, conv_w, conv_b, w_down, tm=512):
    bsz, s, d = x.shape
    d_pool = yp.shape[-1]
    d_attn = q.shape[-1]
    d_ff = w_down.shape[0]
    tpb = s // tm
    n_tiles = bsz * tpb
    n_slots = (tm // Q_BLOCK) * (d_attn // LANES)
    idx = lax.broadcasted_iota(jnp.int32, (K_WIDE, K_WIDE), 0)
    from_here = (idx >= idx.T).astype(bf16)

    def attn_tile(step):
        t = jnp.minimum(step, n_tiles - 1)
        return t // tpb, t % tpb

    def ffn_tile(step):
        t = jnp.maximum(step - 1, 0)
        return t // tpb, t % tpb

    def a_spec(width):
        return pl.BlockSpec((1, tm, width), lambda i: (*attn_tile(i), 0))

    def prev_spec(width):
        per_tile = tm // K_WIDE
        return pl.BlockSpec((1, K_WIDE, width),
                            lambda i: (attn_tile(i)[0], jnp.maximum(attn_tile(i)[1] * per_tile - 1, 0), 0))

    def f_spec(width):
        return pl.BlockSpec((1, tm, width), lambda i: (*ffn_tile(i), 0))

    hbm = pl.BlockSpec(memory_space=pl.ANY)
    return pl.pallas_call(
        functools.partial(_attn_ffn_kernel, tiles_per_batch=tpb, d_pool=d_pool, d_ff=d_ff),
        grid=(n_tiles + 1,),
        in_specs=[
            a_spec(d_attn), a_spec(d_attn), a_spec(d_attn), prev_spec(d_attn), prev_spec(d_attn), hbm, hbm,
            _const_spec(out_g.shape), _const_spec(from_here.shape),
            f_spec(d), f_spec(d_pool),
            pl.BlockSpec((1,) + mod.shape[1:], lambda i: (ffn_tile(i)[0], 0, 0)),
            _const_spec(g2.shape),
            _const_spec(w_out.shape),
            _const_spec(w_up.shape),
            _const_spec(conv_w.shape),
            _const_spec(conv_b.shape),
            _const_spec(w_down.shape),
        ],
        out_specs=f_spec(d),
        out_shape=jax.ShapeDtypeStruct((bsz, s, d), x.dtype),
        scratch_shapes=[
            pltpu.VMEM((tm, d_attn), bf16),
            pltpu.VMEM((2 * d_ff // FF_CHUNK, CONV_HALO, FF_CHUNK), f32),
            pltpu.VMEM((tm, d_ff), bf16),
            pltpu.VMEM((n_slots, 2 * Q_BLOCK, 1), f32),
            pltpu.VMEM((n_slots, 2 * Q_BLOCK, LANES), f32),
            pltpu.VMEM((Q_BLOCK, d_attn), bf16),
            pltpu.VMEM((Q_BLOCK, d_attn), bf16),
            pltpu.SemaphoreType.DMA((2,)),
        ],
        name="attn_convffn",
        compiler_params=pltpu.CompilerParams(
            dimension_semantics=("arbitrary",), vmem_limit_bytes=FUSED_VMEM_LIMIT),
    )(q, k, v, k, v, k, v, out_g, from_here, x, yp, mod, g2, w_out, w_up, conv_w, conv_b, w_down)


def kernel(x, c, ada_w, ada_b, norm1_g, w_in, pool_w, pool_b, pool_scale, q_norm_g, k_norm_g, attn_out_g,
           w_out, norm2_g, w_up, conv_w, conv_b, w_down):
    depth = ada_w.shape[0]
    d = x.shape[-1]
    row = lambda a: a.reshape(1, -1)
    for l in range(depth):
        mod, w_in_b, pool_w_b = _adaln(c, ada_w[l], ada_b[l], (w_in[l], pool_w[l].reshape(-1, pool_w.shape[-1])))
        mod = mod.reshape(c.shape[0], -1, d)
        yp, q, k, v, w_out_b, w_up_b, w_down_b = _inproj(
            x, mod, row(norm1_g[l]), w_in_b, pool_w_b.reshape(pool_w[l].shape), row(pool_b[l]),
            row(pool_scale[l]), row(q_norm_g[l]), row(k_norm_g[l]),
            (w_out[l], w_up[l], w_down[l]))
        x = _attn_ffn(x, yp, q, k, v, mod, attn_out_g[l], row(norm2_g[l]), w_out_b, w_up_b,
                      conv_w[l], row(conv_b[l]), w_down_b)
    return x
```

```python
import collections
import functools
import itertools
import math

import jax
import jax.numpy as jnp
from jax import lax
from jax.experimental import pallas as pl
from jax.experimental.pallas import tpu as pltpu

EPS = 1e-6
HEAD_DIM = 64
POOL_WINDOWS = (2, 4, 8, 16)
POOL_HALO = 16
CONV_HALO = 8
LANES = 128
BF16_SUBLANES = 16
Q_BLOCK = 128
K_WIDE = 256
FF_CHUNK = 256
ATTN_SKEW = 2
LOG_WEIGHT_ZERO = -104.7
LOG2_E = math.log2(math.e)
VMEM_LIMIT = 48 * 1024 * 1024
FUSED_VMEM_LIMIT = 62 * 1024 * 1024

bf16 = jnp.bfloat16
f32 = jnp.float32

_Window = collections.namedtuple("_Window", "k v start width diag prev")


def _const_spec(shape):
    nd = len(shape)
    return pl.BlockSpec(shape, lambda *_: (0,) * nd, pipeline_mode=pl.Buffered(1))


def _adaln_kernel(ct_ref, w_ref, b_ref, *rest):
    n_cast = (len(rest) - 1) // 2
    cast_src, out_ref, cast_dst = rest[:n_cast], rest[n_cast], rest[n_cast + 1:]

    @pl.when(pl.program_id(0) == 0)
    def _():
        out_ref[...] = jnp.broadcast_to(b_ref[...], out_ref.shape)

    ct = ct_ref[...]
    act = ct / (1.0 + jnp.exp(-ct))
    w = w_ref[...]
    rows = []
    for b in range(ct.shape[1]):
        rows.append(jnp.sum(act[:, b:b + 1] * w, axis=0, keepdims=True))
    out_ref[...] += jnp.concatenate(rows, axis=0)

    for src, dst in zip(cast_src, cast_dst):
        dst[...] = src[...].astype(dst.dtype)


def _adaln(c, ada_w, ada_b, later_weights, tk=256):
    bsz, d = c.shape
    n = ada_w.shape[1]
    steps = d // tk

    def row_slice(w):
        rows, cols = w.shape
        assert rows % (steps * BF16_SUBLANES) == 0, w.shape
        return pl.BlockSpec((rows // steps, cols), lambda k: (k, 0))

    cast_specs = [row_slice(w) for w in later_weights]
    return pl.pallas_call(
        _adaln_kernel,
        grid=(steps,),
        in_specs=[
            pl.BlockSpec((tk, bsz), lambda k: (k, 0)),
            pl.BlockSpec((tk, n), lambda k: (k, 0)),
            pl.BlockSpec((1, n), lambda k: (0, 0)),
        ] + cast_specs,
        out_specs=[pl.BlockSpec((bsz, n), lambda k: (0, 0))] + cast_specs,
        out_shape=[jax.ShapeDtypeStruct((bsz, n), f32)]
        + [jax.ShapeDtypeStruct(w.shape, bf16) for w in later_weights],
        name="adaln",
        compiler_params=pltpu.CompilerParams(dimension_semantics=("arbitrary",)),
    )(c.T, ada_w, ada_b.reshape(1, n), *later_weights)


def _inproj_kernel(x_ref, mod_ref, g1_ref, w_in_ref, pool_w_ref, pool_b_ref, pool_s_ref,
                   qg_ref, kg_ref, *rest, d_pool, d_attn):
    n_cast = (len(rest) - 5) // 2
    cast_src = rest[:n_cast]
    yp_ref, q_ref, k_ref, v_ref = rest[n_cast:n_cast + 4]
    cast_dst = rest[n_cast + 4:2 * n_cast + 4]
    halo_ref = rest[-1]
    j = pl.program_id(1)
    tm = x_ref.shape[1]

    @pl.when(j == 0)
    def _():
        halo_ref[...] = jnp.zeros_like(halo_ref)

    x = x_ref[0]
    shift = mod_ref[0, 0:1, :]
    scale = mod_ref[0, 1:2, :]
    inv_rms = lax.rsqrt(jnp.mean(x * x, axis=-1, keepdims=True) + EPS)
    h = (x * inv_rms) * (g1_ref[...] * (1.0 + scale)) + shift
    hb = h.astype(bf16)

    project = lambda lo, hi: jnp.dot(hb, w_in_ref[:, lo:hi], preferred_element_type=f32)
    u = project(0, d_pool)
    tq = project(d_pool, d_pool + d_attn)
    tk = project(d_pool + d_attn, d_pool + 2 * d_attn)
    v_ref[0] = project(d_pool + 2 * d_attn, d_pool + 3 * d_attn).astype(v_ref.dtype)

    ext = jnp.concatenate([halo_ref[...], u], axis=0)
    halo_ref[...] = u[tm - POOL_HALO:, :]
    t1 = (j * tm + 1 + lax.broadcasted_iota(jnp.int32, (tm, 1), 0))
    ys = []
    for g, w in enumerate(POOL_WINDOWS):
        cols = slice(g * LANES, (g + 1) * LANES)
        s = ext[:, cols]
        span = 1
        while span < w:
            s = s + pltpu.roll(s, span, axis=0)
            span *= 2
        count = jnp.minimum(t1, w).astype(f32)
        pooled = s[POOL_HALO:, :] / count - u[:, cols]
        y = jnp.dot(pooled.astype(bf16), pool_w_ref[g], preferred_element_type=f32)
        ys.append((y + pool_b_ref[:, cols]) * pool_s_ref[:, cols])
    yp_ref[0] = jnp.concatenate(ys, axis=-1).astype(yp_ref.dtype)

    head0 = lax.broadcasted_iota(jnp.int32, (1, LANES), 1) < HEAD_DIM

    def head_norm(t, g_ref, post):
        gain = jnp.concatenate([g_ref[...], g_ref[...]], axis=-1) * post
        cols = []
        for c in range(d_attn // LANES):
            tc = t[:, c * LANES:(c + 1) * LANES]
            sq = tc * tc
            ssq0 = jnp.sum(jnp.where(head0, sq, 0.0), axis=-1, keepdims=True)
            ssq1 = jnp.sum(jnp.where(head0, 0.0, sq), axis=-1, keepdims=True)
            inv = jnp.where(head0, lax.rsqrt(ssq0 * (1.0 / HEAD_DIM) + EPS),
                            lax.rsqrt(ssq1 * (1.0 / HEAD_DIM) + EPS))
            cols.append((tc * inv) * gain)
        return jnp.concatenate(cols, axis=-1)

    q_ref[0] = head_norm(tq, qg_ref, -1.0 / math.sqrt(HEAD_DIM)).astype(q_ref.dtype)
    k_ref[0] = head_norm(tk, kg_ref, 1.0).astype(k_ref.dtype)

    for src, dst in zip(cast_src, cast_dst):
        dst[...] = src[...].astype(dst.dtype)


def _inproj(x, mod, g1, w_in, pool_w, pool_b, pool_s, qg, kg, later_weights, tm=1024):
    bsz, s, d = x.shape
    d_pool = pool_b.shape[-1]
    d_attn = (w_in.shape[1] - d_pool) // 3
    n_j = s // tm
    tile = lambda width: pl.BlockSpec((1, tm, width), lambda b, j: (b, j, 0))
    out = jax.ShapeDtypeStruct((bsz, s, d_attn), bf16)

    def row_slice(w):
        rows, cols = w.shape
        n_blocks = bsz * n_j
        while rows % (n_blocks * BF16_SUBLANES):
            assert n_blocks % 2 == 0, w.shape
            n_blocks //= 2
        return pl.BlockSpec((rows // n_blocks, cols), lambda b, j: (jnp.minimum(b * n_j + j, n_blocks - 1), 0))

    cast_specs = [row_slice(w) for w in later_weights]
    return pl.pallas_call(
        functools.partial(_inproj_kernel, d_pool=d_pool, d_attn=d_attn),
        grid=(bsz, n_j),
        in_specs=[
            tile(d),
            pl.BlockSpec((1,) + mod.shape[1:], lambda b, j: (b, 0, 0)),
            _const_spec(g1.shape),
            _const_spec(w_in.shape),
            _const_spec(pool_w.shape),
            _const_spec(pool_b.shape),
            _const_spec(pool_s.shape),
            _const_spec(qg.shape),
            _const_spec(kg.shape),
        ] + cast_specs,
        out_specs=[tile(d_pool), tile(d_attn), tile(d_attn), tile(d_attn)] + cast_specs,
        out_shape=[jax.ShapeDtypeStruct((bsz, s, d_pool), bf16), out, out, out]
        + [jax.ShapeDtypeStruct(w.shape, bf16) for w in later_weights],
        scratch_shapes=[pltpu.VMEM((POOL_HALO, d_pool), f32)],
        name="inproj",
        compiler_params=pltpu.CompilerParams(
            dimension_semantics=("arbitrary", "arbitrary"), vmem_limit_bytes=VMEM_LIMIT),
    )(x, mod, g1, w_in, pool_w, pool_b, pool_s, qg, kg, *later_weights)


def _attn_ffn_kernel(q_ref, kc_ref, vc_ref, kp_ref, vp_ref, k_hbm, v_hbm, og_ref, from_ref,
                     x_ref, yp_ref, mod_ref, g2_ref, w_out_ref, w_up_ref, cw_ref, cb_ref, w_down_ref,
                     out_ref,
                     o_scr, halo_ref, act_ref, run_ref, acc_ref, kbuf, vbuf, sem,
                     *, tiles_per_batch, d_pool, d_ff):
    s = pl.program_id(0)
    n_tiles = pl.num_programs(0) - 1
    ta = jnp.minimum(s, n_tiles - 1)
    ta_in_batch = lax.rem(ta, tiles_per_batch)
    batch_a = lax.div(ta, tiles_per_batch)
    tf_in_batch = lax.rem(jnp.maximum(s - 1, 0), tiles_per_batch)
    tm = q_ref.shape[1]
    n_qblk = tm // Q_BLOCK
    n_cols = q_ref.shape[2] // LANES

    @pl.when(s == 0)
    def _():
        o_scr[...] = jnp.zeros_like(o_scr)

    @pl.when(tf_in_batch == 0)
    def _():
        halo_ref[...] = jnp.zeros_like(halo_ref)

    lane = lax.broadcasted_iota(jnp.int32, (1, LANES), 1)
    head0 = lane < HEAD_DIM
    prev_valid = ta_in_batch > 0

    def rows(ref, start, width, c):
        cols = slice(c * LANES, (c + 1) * LANES)
        if len(ref.shape) == 3:
            return ref[0, start:start + width, cols]
        return ref[start:start + width, cols]

    def stacked_q(r, c):
        q = q_ref[0, r * Q_BLOCK:(r + 1) * Q_BLOCK, c * LANES:(c + 1) * LANES]
        zq = jnp.zeros_like(q)
        return jnp.concatenate([jnp.where(head0, q, zq), jnp.where(head0, zq, q)], axis=0)

    def scores(c, q2, win):
        kb = rows(win.k, win.start, win.width, c)
        return lax.dot_general(q2, kb, (((1,), (1,)), ((), ())), preferred_element_type=f32)

    def log_terms(zn, win):
        width = zn.shape[1]
        log_1m = jnp.minimum(zn, 0.0) - jnp.log(1.0 + jnp.exp2(jnp.abs(zn) * -LOG2_E))
        mask = None
        if win.diag:
            row = lax.broadcasted_iota(jnp.int32, zn.shape, 0) & (Q_BLOCK - 1)
            mask = lax.broadcasted_iota(jnp.int32, zn.shape, 1) < row
        elif win.prev:
            mask = prev_valid
        if mask is not None:
            log_1m = jnp.where(mask, log_1m, 0.0)
        from_here = jnp.dot(log_1m.astype(bf16), from_ref[0:width, 0:width], preferred_element_type=f32)
        return from_here - zn, jnp.sum(log_1m, axis=1, keepdims=True), mask

    def weighted_values(c, win, log_w, mask, run):
        a = jnp.exp(log_w if run is None else log_w + run)
        if mask is not None:
            a = jnp.where(mask, a, 0.0)
        return jnp.dot(a.astype(bf16), rows(win.v, win.start, win.width, c), preferred_element_type=f32)

    def visit(r, windows, fresh):
        halves = (range(0, n_cols // 2), range(n_cols // 2, n_cols))
        zs, terms = {}, {}
        for half in halves:
            for c in half:
                q2 = stacked_q(r, c)
                zs[c] = [scores(c, q2, w) for w in windows]
            yield
        for half in halves:
            for c in half:
                terms[c] = [log_terms(z, w) for z, w in zip(zs[c], windows)]
            yield
        for half in halves:
            for c in half:
                slot = r * n_cols + c
                run, acc = (None, None) if fresh else (run_ref[slot], acc_ref[slot])
                for w, (log_w, row_sum, mask) in zip(windows, terms[c]):
                    pv = weighted_values(c, w, log_w, mask, run)
                    acc = pv if acc is None else acc + pv
                    run = row_sum if run is None else run + row_sum
                run_ref[slot] = run
                acc_ref[slot] = acc
                if fresh:
                    normalise(r, c, acc)
            yield

    def normalise(r, c, acc):
        cols = slice(c * LANES, (c + 1) * LANES)
        o = jnp.where(head0, acc[0:Q_BLOCK, :], acc[Q_BLOCK:, :])
        o2 = o * o
        ssq0 = jnp.sum(jnp.where(head0, o2, 0.0), axis=-1, keepdims=True)
        ssq1 = jnp.sum(jnp.where(head0, 0.0, o2), axis=-1, keepdims=True)
        ms = jnp.where(head0, ssq0, ssq1) * (1.0 / HEAD_DIM)
        gain = jnp.concatenate([og_ref[2 * c:2 * c + 1, :], og_ref[2 * c + 1:2 * c + 2, :]], axis=-1)
        o_scr[r * Q_BLOCK:(r + 1) * Q_BLOCK, cols] = (o * lax.rsqrt(ms + EPS) * gain).astype(o_scr.dtype)

    def first_windows(r):
        wins = [_Window(kc_ref, vc_ref, r * Q_BLOCK, Q_BLOCK, True, False)]
        start = r * Q_BLOCK - K_WIDE
        if start >= 0:
            wins.append(_Window(kc_ref, vc_ref, start, K_WIDE, False, False))
        else:
            if r > 0:
                wins.append(_Window(kc_ref, vc_ref, 0, r * Q_BLOCK, False, False))
            wins.append(_Window(kp_ref, vp_ref, K_WIDE + start, -start, False, True))
        return wins

    def attention_main():
        blocks = [visit(r, first_windows(r), True) for r in range(n_qblk)]
        slots = sorted((2 * r + 2 * ATTN_SKEW * m + h, -r) for r in range(n_qblk) for m in range(3) for h in range(2))
        for _, neg_r in slots:
            next(blocks[-neg_r])
            yield

    gate1 = mod_ref[0, 2:3, :]
    shift = mod_ref[0, 3:4, :]
    scale = mod_ref[0, 4:5, :]
    gate2 = mod_ref[0, 5:6, :]
    n_chunks = d_ff // FF_CHUNK

    def ffn_main():
        mixed = (jnp.dot(yp_ref[0], w_out_ref[0:d_pool, :], preferred_element_type=f32)
                 + jnp.dot(o_scr[...], w_out_ref[d_pool:, :], preferred_element_type=f32))
        x1 = x_ref[0] + gate1 * mixed
        inv_rms = lax.rsqrt(jnp.mean(x1 * x1, axis=-1, keepdims=True) + EPS)
        hb = ((x1 * inv_rms) * (g2_ref[...] * (1.0 + scale)) + shift).astype(bf16)
        yield

        def conv(c, cols):
            up = jnp.dot(hb, w_up_ref[:, cols], preferred_element_type=f32)
            ext = jnp.concatenate([halo_ref[c], up], axis=0)
            halo_ref[c] = up[tm - CONV_HALO:, :]
            prev1 = pltpu.roll(ext, 1, axis=0)[CONV_HALO:, :]
            prev2 = pltpu.roll(ext, 2, axis=0)[CONV_HALO:, :]
            return (cw_ref[2:3, cols] * up + cw_ref[1:2, cols] * prev1
                    + cw_ref[0:1, cols] * prev2 + cb_ref[:, cols])

        for c in range(n_chunks):
            gate = conv(c, slice(c * FF_CHUNK, (c + 1) * FF_CHUNK))
            yield
            val = conv(n_chunks + c, slice(d_ff + c * FF_CHUNK, d_ff + (c + 1) * FF_CHUNK))
            act = gate / (1.0 + jnp.exp2(gate * -LOG2_E)) * val
            act_ref[:, c * FF_CHUNK:(c + 1) * FF_CHUNK] = act.astype(bf16)
            yield
        ffn = jnp.dot(act_ref[...], w_down_ref[...], preferred_element_type=f32)
        out_ref[0] = x1 + gate2 * ffn
        yield

    for _ in itertools.zip_longest(attention_main(), ffn_main()):
        pass

    def unfinished():
        return jnp.max(run_ref[...]) > LOG_WEIGHT_ZERO

    def drain(gen):
        for _ in gen:
            pass

    go = unfinished()

    @pl.when(go)
    def _():
        for r in range(1, n_qblk):
            wins = []
            seen = r * Q_BLOCK - K_WIDE
            if seen > 0:
                wins.append(_Window(kc_ref, vc_ref, 0, seen, False, False))
            top = K_WIDE + min(seen, 0)
            wins.append(_Window(kp_ref, vp_ref, 0, top, False, True))
            drain(visit(r, wins, False))

    def cond(state):
        jb, more = state
        return jnp.logical_and(jb >= 0, more)

    def body(state):
        jb, _ = state
        src = pl.ds(pl.multiple_of(jb * Q_BLOCK, Q_BLOCK), Q_BLOCK)
        copies = [pltpu.make_async_copy(k_hbm.at[batch_a, src, :], kbuf, sem.at[0]),
                  pltpu.make_async_copy(v_hbm.at[batch_a, src, :], vbuf, sem.at[1])]
        for cp in copies:
            cp.start()
        for cp in copies:
            cp.wait()
        for r in range(n_qblk):
            drain(visit(r, [_Window(kbuf, vbuf, 0, Q_BLOCK, False, False)], False))
        return jb - 1, unfinished()

    first = ta_in_batch * n_qblk - (K_WIDE // Q_BLOCK + 1)
    lax.while_loop(cond, body, (first, go))

    @pl.when(go)
    def _():
        for r in range(n_qblk):
            for c in range(n_cols):
                normalise(r, c, acc_ref[r * n_cols + c])


def _attn_ffn(x, yp, q, k, v, mod, out_g, g2, w_out, w_up, conv_w, conv_b, w_down, tm=512):
    bsz, s, d = x.shape
    d_pool = yp.shape[-1]
    d_attn = q.shape[-1]
    d_ff = w_down.shape[0]
    tpb = s // tm
    n_tiles = bsz * tpb
    n_slots = (tm // Q_BLOCK) * (d_attn // LANES)
    idx = lax.broadcasted_iota(jnp.int32, (K_WIDE, K_WIDE), 0)
    from_here = (idx >= idx.T).astype(bf16)

    def attn_tile(step):
        t = jnp.minimum(step, n_tiles - 1)
        return t // tpb, t % tpb

    def ffn_tile(step):
        t = jnp.maximum(step - 1, 0)
        return t // tpb, t % tpb

    def a_spec(width):
        return pl.BlockSpec((1, tm, width), lambda i: (*attn_tile(i), 0))

    def prev_spec(width):
        per_tile = tm // K_WIDE
        return pl.BlockSpec((1, K_WIDE, width),
                            lambda i: (attn_tile(i)[0], jnp.maximum(attn_tile(i)[1] * per_tile - 1, 0), 0))

    def f_spec(width):
        return pl.BlockSpec((1, tm, width), lambda i: (*ffn_tile(i), 0))

    hbm = pl.BlockSpec(memory_space=pl.ANY)
    return pl.pallas_call(
        functools.partial(_attn_ffn_kernel, tiles_per_batch=tpb, d_pool=d_pool, d_ff=d_ff),
        grid=(n_tiles + 1,),
        in_specs=[
            a_spec(d_attn), a_spec(d_attn), a_spec(d_attn), prev_spec(d_attn), prev_spec(d_attn), hbm, hbm,
            _const_spec(out_g.shape), _const_spec(from_here.shape),
            f_spec(d), f_spec(d_pool),
            pl.BlockSpec((1,) + mod.shape[1:], lambda i: (ffn_tile(i)[0], 0, 0)),
            _const_spec(g2.shape),
            _const_spec(w_out.shape),
            _const_spec(w_up.shape),
            _const_spec(conv_w.shape),
            _const_spec(conv_b.shape),
            _const_spec(w_down.shape),
        ],
        out_specs=f_spec(d),
        out_shape=jax.ShapeDtypeStruct((bsz, s, d), x.dtype),
        scratch_shapes=[
            pltpu.VMEM((tm, d_attn), bf16),
            pltpu.VMEM((2 * d_ff // FF_CHUNK, CONV_HALO, FF_CHUNK), f32),
            pltpu.VMEM((tm, d_ff), bf16),
            pltpu.VMEM((n_slots, 2 * Q_BLOCK, 1), f32),
            pltpu.VMEM((n_slots, 2 * Q_BLOCK, LANES), f32),
            pltpu.VMEM((Q_BLOCK, d_attn), bf16),
            pltpu.VMEM((Q_BLOCK, d_attn), bf16),
            pltpu.SemaphoreType.DMA((2,)),
        ],
        name="attn_convffn",
        compiler_params=pltpu.CompilerParams(
            dimension_semantics=("arbitrary",), vmem_limit_bytes=FUSED_VMEM_LIMIT),
    )(q, k, v, k, v, k, v, out_g, from_here, x, yp, mod, g2, w_out, w_up, conv_w, conv_b, w_down)


def kernel(x, c, ada_w, ada_b, norm1_g, w_in, pool_w, pool_b, pool_scale, q_norm_g, k_norm_g, attn_out_g,
           w_out, norm2_g, w_up, conv_w, conv_b, w_down):
    depth = ada_w.shape[0]
    d = x.shape[-1]
    row = lambda a: a.reshape(1, -1)
    for l in range(depth):
        mod, w_in_b, pool_w_b = _adaln(c, ada_w[l], ada_b[l], (w_in[l], pool_w[l].reshape(-1, pool_w.shape[-1])))
        mod = mod.reshape(c.shape[0], -1, d)
        yp, q, k, v, w_out_b, w_up_b, w_down_b = _inproj(
            x, mod, row(norm1_g[l]), w_in_b, pool_w_b.reshape(pool_w[l].shape), row(pool_b[l]),
            row(pool_scale[l]), row(q_norm_g[l]), row(k_norm_g[l]),
            (w_out[l], w_up[l], w_down[l]))
        x = _attn_ffn(x, yp, q, k, v, mod, attn_out_g[l], row(norm2_g[l]), w_out_b, w_up_b,
                      conv_w[l], row(conv_b[l]), w_down_b)
    return x
```

```python
import collections
import functools
import itertools
import math

import jax
import jax.numpy as jnp
from jax import lax
from jax.experimental import pallas as pl
from jax.experimental.pallas import tpu as pltpu

EPS = 1e-6
HEAD_DIM = 64
POOL_WINDOWS = (2, 4, 8, 16)
POOL_HALO = 16
CONV_HALO = 8
LANES = 128
BF16_SUBLANES = 16
Q_BLOCK = 128
K_WIDE = 256
FF_CHUNK = 256
ATTN_SKEW = 3
LOG_WEIGHT_ZERO = -104.7
LOG2_E = math.log2(math.e)
VMEM_LIMIT = 48 * 1024 * 1024
FUSED_VMEM_LIMIT = 62 * 1024 * 1024

bf16 = jnp.bfloat16
f32 = jnp.float32

_Window = collections.namedtuple("_Window", "k v start width diag prev")


def _const_spec(shape):
    nd = len(shape)
    return pl.BlockSpec(shape, lambda *_: (0,) * nd, pipeline_mode=pl.Buffered(1))


def _adaln_kernel(ct_ref, w_ref, b_ref, *rest):
    n_cast = (len(rest) - 1) // 2
    cast_src, out_ref, cast_dst = rest[:n_cast], rest[n_cast], rest[n_cast + 1:]

    @pl.when(pl.program_id(0) == 0)
    def _():
        out_ref[...] = jnp.broadcast_to(b_ref[...], out_ref.shape)

    ct = ct_ref[...]
    act = ct / (1.0 + jnp.exp(-ct))
    w = w_ref[...]
    rows = []
    for b in range(ct.shape[1]):
        rows.append(jnp.sum(act[:, b:b + 1] * w, axis=0, keepdims=True))
    out_ref[...] += jnp.concatenate(rows, axis=0)

    for src, dst in zip(cast_src, cast_dst):
        dst[...] = src[...].astype(dst.dtype)


def _adaln(c, ada_w, ada_b, later_weights, tk=256):
    bsz, d = c.shape
    n = ada_w.shape[1]
    steps = d // tk

    def row_slice(w):
        rows, cols = w.shape
        assert rows % (steps * BF16_SUBLANES) == 0, w.shape
        return pl.BlockSpec((rows // steps, cols), lambda k: (k, 0))

    cast_specs = [row_slice(w) for w in later_weights]
    return pl.pallas_call(
        _adaln_kernel,
        grid=(steps,),
        in_specs=[
            pl.BlockSpec((tk, bsz), lambda k: (k, 0)),
            pl.BlockSpec((tk, n), lambda k: (k, 0)),
            pl.BlockSpec((1, n), lambda k: (0, 0)),
        ] + cast_specs,
        out_specs=[pl.BlockSpec((bsz, n), lambda k: (0, 0))] + cast_specs,
        out_shape=[jax.ShapeDtypeStruct((bsz, n), f32)]
        + [jax.ShapeDtypeStruct(w.shape, bf16) for w in later_weights],
        name="adaln",
        compiler_params=pltpu.CompilerParams(dimension_semantics=("arbitrary",)),
    )(c.T, ada_w, ada_b.reshape(1, n), *later_weights)


def _inproj_kernel(x_ref, mod_ref, g1_ref, w_in_ref, pool_w_ref, pool_b_ref, pool_s_ref,
                   qg_ref, kg_ref, *rest, d_pool, d_attn):
    n_cast = (len(rest) - 5) // 2
    cast_src = rest[:n_cast]
    yp_ref, q_ref, k_ref, v_ref = rest[n_cast:n_cast + 4]
    cast_dst = rest[n_cast + 4:2 * n_cast + 4]
    halo_ref = rest[-1]
    j = pl.program_id(1)
    tm = x_ref.shape[1]

    @pl.when(j == 0)
    def _():
        halo_ref[...] = jnp.zeros_like(halo_ref)

    x = x_ref[0]
    shift = mod_ref[0, 0:1, :]
    scale = mod_ref[0, 1:2, :]
    inv_rms = lax.rsqrt(jnp.mean(x * x, axis=-1, keepdims=True) + EPS)
    h = (x * inv_rms) * (g1_ref[...] * (1.0 + scale)) + shift
    hb = h.astype(bf16)

    project = lambda lo, hi: jnp.dot(hb, w_in_ref[:, lo:hi], preferred_element_type=f32)
    u = project(0, d_pool)
    tq = project(d_pool, d_pool + d_attn)
    tk = project(d_pool + d_attn, d_pool + 2 * d_attn)
    v_ref[0] = project(d_pool + 2 * d_attn, d_pool + 3 * d_attn).astype(v_ref.dtype)

    ext = jnp.concatenate([halo_ref[...], u], axis=0)
    halo_ref[...] = u[tm - POOL_HALO:, :]
    t1 = (j * tm + 1 + lax.broadcasted_iota(jnp.int32, (tm, 1), 0))
    ys = []
    for g, w in enumerate(POOL_WINDOWS):
        cols = slice(g * LANES, (g + 1) * LANES)
        s = ext[:, cols]
        span = 1
        while span < w:
            s = s + pltpu.roll(s, span, axis=0)
            span *= 2
        count = jnp.minimum(t1, w).astype(f32)
        pooled = s[POOL_HALO:, :] / count - u[:, cols]
        y = jnp.dot(pooled.astype(bf16), pool_w_ref[g], preferred_element_type=f32)
        ys.append((y + pool_b_ref[:, cols]) * pool_s_ref[:, cols])
    yp_ref[0] = jnp.concatenate(ys, axis=-1).astype(yp_ref.dtype)

    head0 = lax.broadcasted_iota(jnp.int32, (1, LANES), 1) < HEAD_DIM

    def head_norm(t, g_ref, post):
        gain = jnp.concatenate([g_ref[...], g_ref[...]], axis=-1) * post
        cols = []
        for c in range(d_attn // LANES):
            tc = t[:, c * LANES:(c + 1) * LANES]
            sq = tc * tc
            ssq0 = jnp.sum(jnp.where(head0, sq, 0.0), axis=-1, keepdims=True)
            ssq1 = jnp.sum(jnp.where(head0, 0.0, sq), axis=-1, keepdims=True)
            inv = jnp.where(head0, lax.rsqrt(ssq0 * (1.0 / HEAD_DIM) + EPS),
                            lax.rsqrt(ssq1 * (1.0 / HEAD_DIM) + EPS))
            cols.append((tc * inv) * gain)
        return jnp.concatenate(cols, axis=-1)

    q_ref[0] = head_norm(tq, qg_ref, -1.0 / math.sqrt(HEAD_DIM)).astype(q_ref.dtype)
    k_ref[0] = head_norm(tk, kg_ref, 1.0).astype(k_ref.dtype)

    for src, dst in zip(cast_src, cast_dst):
        dst[...] = src[...].astype(dst.dtype)


def _inproj(x, mod, g1, w_in, pool_w, pool_b, pool_s, qg, kg, later_weights, tm=1024):
    bsz, s, d = x.shape
    d_pool = pool_b.shape[-1]
    d_attn = (w_in.shape[1] - d_pool) // 3
    n_j = s // tm
    tile = lambda width: pl.BlockSpec((1, tm, width), lambda b, j: (b, j, 0))
    out = jax.ShapeDtypeStruct((bsz, s, d_attn), bf16)

    def row_slice(w):
        rows, cols = w.shape
        n_blocks = bsz * n_j
        while rows % (n_blocks * BF16_SUBLANES):
            assert n_blocks % 2 == 0, w.shape
            n_blocks //= 2
        return pl.BlockSpec((rows // n_blocks, cols), lambda b, j: (jnp.minimum(b * n_j + j, n_blocks - 1), 0))

    cast_specs = [row_slice(w) for w in later_weights]
    return pl.pallas_call(
        functools.partial(_inproj_kernel, d_pool=d_pool, d_attn=d_attn),
        grid=(bsz, n_j),
        in_specs=[
            tile(d),
            pl.BlockSpec((1,) + mod.shape[1:], lambda b, j: (b, 0, 0)),
            _const_spec(g1.shape),
            _const_spec(w_in.shape),
            _const_spec(pool_w.shape),
            _const_spec(pool_b.shape),
            _const_spec(pool_s.shape),
            _const_spec(qg.shape),
            _const_spec(kg.shape),
        ] + cast_specs,
        out_specs=[tile(d_pool), tile(d_attn), tile(d_attn), tile(d_attn)] + cast_specs,
        out_shape=[jax.ShapeDtypeStruct((bsz, s, d_pool), bf16), out, out, out]
        + [jax.ShapeDtypeStruct(w.shape, bf16) for w in later_weights],
        scratch_shapes=[pltpu.VMEM((POOL_HALO, d_pool), f32)],
        name="inproj",
        compiler_params=pltpu.CompilerParams(
            dimension_semantics=("arbitrary", "arbitrary"), vmem_limit_bytes=VMEM_LIMIT),
    )(x, mod, g1, w_in, pool_w, pool_b, pool_s, qg, kg, *later_weights)


def _attn_ffn_kernel(q_ref, kc_ref, vc_ref, kp_ref, vp_ref, k_hbm, v_hbm, og_ref, from_ref,
                     x_ref, yp_ref, mod_ref, g2_ref, w_out_ref, w_up_ref, cw_ref, cb_ref, w_down_ref,
                     out_ref,
                     o_scr, halo_ref, act_ref, run_ref, acc_ref, kbuf, vbuf, sem,
                     *, tiles_per_batch, d_pool, d_ff):
    s = pl.program_id(0)
    n_tiles = pl.num_programs(0) - 1
    ta = jnp.minimum(s, n_tiles - 1)
    ta_in_batch = lax.rem(ta, tiles_per_batch)
    batch_a = lax.div(ta, tiles_per_batch)
    tf_in_batch = lax.rem(jnp.maximum(s - 1, 0), tiles_per_batch)
    tm = q_ref.shape[1]
    n_qblk = tm // Q_BLOCK
    n_cols = q_ref.shape[2] // LANES

    @pl.when(s == 0)
    def _():
        o_scr[...] = jnp.zeros_like(o_scr)

    @pl.when(tf_in_batch == 0)
    def _():
        halo_ref[...] = jnp.zeros_like(halo_ref)

    lane = lax.broadcasted_iota(jnp.int32, (1, LANES), 1)
    head0 = lane < HEAD_DIM
    prev_valid = ta_in_batch > 0

    def rows(ref, start, width, c):
        cols = slice(c * LANES, (c + 1) * LANES)
        if len(ref.shape) == 3:
            return ref[0, start:start + width, cols]
        return ref[start:start + width, cols]

    def stacked_q(r, c):
        q = q_ref[0, r * Q_BLOCK:(r + 1) * Q_BLOCK, c * LANES:(c + 1) * LANES]
        zq = jnp.zeros_like(q)
        return jnp.concatenate([jnp.where(head0, q, zq), jnp.where(head0, zq, q)], axis=0)

    def scores(c, q2, win):
        kb = rows(win.k, win.start, win.width, c)
        return lax.dot_general(q2, kb, (((1,), (1,)), ((), ())), preferred_element_type=f32)

    def log_terms(zn, win):
        width = zn.shape[1]
        log_1m = jnp.minimum(zn, 0.0) - jnp.log(1.0 + jnp.exp2(jnp.abs(zn) * -LOG2_E))
        mask = None
        if win.diag:
            row = lax.broadcasted_iota(jnp.int32, zn.shape, 0) & (Q_BLOCK - 1)
            mask = lax.broadcasted_iota(jnp.int32, zn.shape, 1) < row
        elif win.prev:
            mask = prev_valid
        if mask is not None:
            log_1m = jnp.where(mask, log_1m, 0.0)
        from_here = jnp.dot(log_1m.astype(bf16), from_ref[0:width, 0:width], preferred_element_type=f32)
        return from_here - zn, jnp.sum(log_1m, axis=1, keepdims=True), mask

    def weighted_values(c, win, log_w, mask, run):
        a = jnp.exp(log_w if run is None else log_w + run)
        if mask is not None:
            a = jnp.where(mask, a, 0.0)
        return jnp.dot(a.astype(bf16), rows(win.v, win.start, win.width, c), preferred_element_type=f32)

    def visit(r, windows, fresh):
        halves = (range(0, n_cols // 2), range(n_cols // 2, n_cols))
        zs, terms = {}, {}
        for half in halves:
            for c in half:
                q2 = stacked_q(r, c)
                zs[c] = [scores(c, q2, w) for w in windows]
            yield
        for half in halves:
            for c in half:
                terms[c] = [log_terms(z, w) for z, w in zip(zs[c], windows)]
            yield
        for half in halves:
            for c in half:
                slot = r * n_cols + c
                run, acc = (None, None) if fresh else (run_ref[slot], acc_ref[slot])
                for w, (log_w, row_sum, mask) in zip(windows, terms[c]):
                    pv = weighted_values(c, w, log_w, mask, run)
                    acc = pv if acc is None else acc + pv
                    run = row_sum if run is None else run + row_sum
                run_ref[slot] = run
                acc_ref[slot] = acc
                if fresh:
                    normalise(r, c, acc)
            yield

    def normalise(r, c, acc):
        cols = slice(c * LANES, (c + 1) * LANES)
        o = jnp.where(head0, acc[0:Q_BLOCK, :], acc[Q_BLOCK:, :])
        o2 = o * o
        ssq0 = jnp.sum(jnp.where(head0, o2, 0.0), axis=-1, keepdims=True)
        ssq1 = jnp.sum(jnp.where(head0, 0.0, o2), axis=-1, keepdims=True)
        ms = jnp.where(head0, ssq0, ssq1) * (1.0 / HEAD_DIM)
        gain = jnp.concatenate([og_ref[2 * c:2 * c + 1, :], og_ref[2 * c + 1:2 * c + 2, :]], axis=-1)
        o_scr[r * Q_BLOCK:(r + 1) * Q_BLOCK, cols] = (o * lax.rsqrt(ms + EPS) * gain).astype(o_scr.dtype)

    def first_windows(r):
        wins = [_Window(kc_ref, vc_ref, r * Q_BLOCK, Q_BLOCK, True, False)]
        start = r * Q_BLOCK - K_WIDE
        if start >= 0:
            wins.append(_Window(kc_ref, vc_ref, start, K_WIDE, False, False))
        else:
            if r > 0:
                wins.append(_Window(kc_ref, vc_ref, 0, r * Q_BLOCK, False, False))
            wins.append(_Window(kp_ref, vp_ref, K_WIDE + start, -start, False, True))
        return wins

    def attention_main():
        blocks = [visit(r, first_windows(r), True) for r in range(n_qblk)]
        slots = sorted((2 * r + 2 * ATTN_SKEW * m + h, -r) for r in range(n_qblk) for m in range(3) for h in range(2))
        for _, neg_r in slots:
            next(blocks[-neg_r])
            yield

    gate1 = mod_ref[0, 2:3, :]
    shift = mod_ref[0, 3:4, :]
    scale = mod_ref[0, 4:5, :]
    gate2 = mod_ref[0, 5:6, :]
    n_chunks = d_ff // FF_CHUNK

    def ffn_main():
        mixed = (jnp.dot(yp_ref[0], w_out_ref[0:d_pool, :], preferred_element_type=f32)
                 + jnp.dot(o_scr[...], w_out_ref[d_pool:, :], preferred_element_type=f32))
        x1 = x_ref[0] + gate1 * mixed
        inv_rms = lax.rsqrt(jnp.mean(x1 * x1, axis=-1, keepdims=True) + EPS)
        hb = ((x1 * inv_rms) * (g2_ref[...] * (1.0 + scale)) + shift).astype(bf16)
        yield

        def conv(c, cols):
            up = jnp.dot(hb, w_up_ref[:, cols], preferred_element_type=f32)
            ext = jnp.concatenate([halo_ref[c], up], axis=0)
            halo_ref[c] = up[tm - CONV_HALO:, :]
            prev1 = pltpu.roll(ext, 1, axis=0)[CONV_HALO:, :]
            prev2 = pltpu.roll(ext, 2, axis=0)[CONV_HALO:, :]
            return (cw_ref[2:3, cols] * up + cw_ref[1:2, cols] * prev1
                    + cw_ref[0:1, cols] * prev2 + cb_ref[:, cols])

        for c in range(n_chunks):
            gate = conv(c, slice(c * FF_CHUNK, (c + 1) * FF_CHUNK))
            yield
            val = conv(n_chunks + c, slice(d_ff + c * FF_CHUNK, d_ff + (c + 1) * FF_CHUNK))
            act = gate / (1.0 + jnp.exp2(gate * -LOG2_E)) * val
            act_ref[:, c * FF_CHUNK:(c + 1) * FF_CHUNK] = act.astype(bf16)
            yield
        ffn = jnp.dot(act_ref[...], w_down_ref[...], preferred_element_type=f32)
        out_ref[0] = x1 + gate2 * ffn
        yield

    for _ in itertools.zip_longest(attention_main(), ffn_main()):
        pass

    def unfinished():
        return jnp.max(run_ref[...]) > LOG_WEIGHT_ZERO

    def drain(gen):
        for _ in gen:
            pass

    go = unfinished()

    @pl.when(go)
    def _():
        for r in range(1, n_qblk):
            wins = []
            seen = r * Q_BLOCK - K_WIDE
            if seen > 0:
                wins.append(_Window(kc_ref, vc_ref, 0, seen, False, False))
            top = K_WIDE + min(seen, 0)
            wins.append(_Window(kp_ref, vp_ref, 0, top, False, True))
            drain(visit(r, wins, False))

    def cond(state):
        jb, more = state
        return jnp.logical_and(jb >= 0, more)

    def body(state):
        jb, _ = state
        src = pl.ds(pl.multiple_of(jb * Q_BLOCK, Q_BLOCK), Q_BLOCK)
        copies = [pltpu.make_async_copy(k_hbm.at[batch_a, src, :], kbuf, sem.at[0]),
                  pltpu.make_async_copy(v_hbm.at[batch_a, src, :], vbuf, sem.at[1])]
        for cp in copies:
            cp.start()
        for cp in copies:
            cp.wait()
        for r in range(n_qblk):
            drain(visit(r, [_Window(kbuf, vbuf, 0, Q_BLOCK, False, False)], False))
        return jb - 1, unfinished()

    first = ta_in_batch * n_qblk - (K_WIDE // Q_BLOCK + 1)
    lax.while_loop(cond, body, (first, go))

    @pl.when(go)
    def _():
        for r in range(n_qblk):
            for c in range(n_cols):
                normalise(r, c, acc_ref[r * n_cols + c])


def _attn_ffn(x, yp, q, k, v, mod, out_g, g2, w_out, w_up, conv_w, conv_b, w_down, tm=512):
    bsz, s, d = x.shape
    d_pool = yp.shape[-1]
    d_attn = q.shape[-1]
    d_ff = w_down.shape[0]
    tpb = s // tm
    n_tiles = bsz * tpb
    n_slots = (tm // Q_BLOCK) * (d_attn // LANES)
    idx = lax.broadcasted_iota(jnp.int32, (K_WIDE, K_WIDE), 0)
    from_here = (idx >= idx.T).astype(bf16)

    def attn_tile(step):
        t = jnp.minimum(step, n_tiles - 1)
        return t // tpb, t % tpb

    def ffn_tile(step):
        t = jnp.maximum(step - 1, 0)
        return t // tpb, t % tpb

    def a_spec(width):
        return pl.BlockSpec((1, tm, width), lambda i: (*attn_tile(i), 0))

    def prev_spec(width):
        per_tile = tm // K_WIDE
        return pl.BlockSpec((1, K_WIDE, width),
                            lambda i: (attn_tile(i)[0], jnp.maximum(attn_tile(i)[1] * per_tile - 1, 0), 0))

    def f_spec(width):
        return pl.BlockSpec((1, tm, width), lambda i: (*ffn_tile(i), 0))

    hbm = pl.BlockSpec(memory_space=pl.ANY)
    return pl.pallas_call(
        functools.partial(_attn_ffn_kernel, tiles_per_batch=tpb, d_pool=d_pool, d_ff=d_ff),
        grid=(n_tiles + 1,),
        in_specs=[
            a_spec(d_attn), a_spec(d_attn), a_spec(d_attn), prev_spec(d_attn), prev_spec(d_attn), hbm, hbm,
            _const_spec(out_g.shape), _const_spec(from_here.shape),
            f_spec(d), f_spec(d_pool),
            pl.BlockSpec((1,) + mod.shape[1:], lambda i: (ffn_tile(i)[0], 0, 0)),
            _const_spec(g2.shape),
            _const_spec(w_out.shape),
            _const_spec(w_up.shape),
            _const_spec(conv_w.shape),
            _const_spec(conv_b.shape),
            _const_spec(w_down.shape),
        ],
        out_specs=f_spec(d),
        out_shape=jax.ShapeDtypeStruct((bsz, s, d), x.dtype),
        scratch_shapes=[
            pltpu.VMEM((tm, d_attn), bf16),
            pltpu.VMEM((2 * d_ff // FF_CHUNK, CONV_HALO, FF_CHUNK), f32),
            pltpu.VMEM((tm, d_ff), bf16),
            pltpu.VMEM((n_slots, 2 * Q_BLOCK, 1), f32),
            pltpu.VMEM((n_slots, 2 * Q_BLOCK, LANES), f32),
            pltpu.VMEM((Q_BLOCK, d_attn), bf16),
            pltpu.VMEM((Q_BLOCK, d_attn), bf16),
            pltpu.SemaphoreType.DMA((2,)),
        ],
        name="attn_convffn",
        compiler_params=pltpu.CompilerParams(
            dimension_semantics=("arbitrary",), vmem_limit_bytes=FUSED_VMEM_LIMIT),
    )(q, k, v, k, v, k, v, out_g, from_here, x, yp, mod, g2, w_out, w_up, conv_w, conv_b, w_down)


def kernel(x, c, ada_w, ada_b, norm1_g, w_in, pool_w, pool_b, pool_scale, q_norm_g, k_norm_g, attn_out_g,
           w_out, norm2_g, w_up, conv_w, conv_b, w_down):
    depth = ada_w.shape[0]
    d = x.shape[-1]
    row = lambda a: a.reshape(1, -1)
    for l in range(depth):
        mod, w_in_b, pool_w_b = _adaln(c, ada_w[l], ada_b[l], (w_in[l], pool_w[l].reshape(-1, pool_w.shape[-1])))
        mod = mod.reshape(c.shape[0], -1, d)
        yp, q, k, v, w_out_b, w_up_b, w_down_b = _inproj(
            x, mod, row(norm1_g[l]), w_in_b, pool_w_b.reshape(pool_w[l].shape), row(pool_b[l]),
            row(pool_scale[l]), row(q_norm_g[l]), row(k_norm_g[l]),
            (w_out[l], w_up[l], w_down[l]))
        x = _attn_ffn(x, yp, q, k, v, mod, attn_out_g[l], row(norm2_g[l]), w_out_b, w_up_b,
                      conv_w[l], row(conv_b[l]), w_down_b)
    return x
```

```python
import collections
import functools
import itertools
import math

import jax
import jax.numpy as jnp
from jax import lax
from jax.experimental import pallas as pl
from jax.experimental.pallas import tpu as pltpu

EPS = 1e-6
HEAD_DIM = 64
POOL_WINDOWS = (2, 4, 8, 16)
POOL_HALO = 16
CONV_HALO = 8
LANES = 128
BF16_SUBLANES = 16
Q_BLOCK = 128
K_WIDE = 256
FF_CHUNK = 256
ATTN_SKEW = 3
LOG_WEIGHT_ZERO = -104.7
LOG2_E = math.log2(math.e)
VMEM_LIMIT = 48 * 1024 * 1024
FUSED_VMEM_LIMIT = 62 * 1024 * 1024

bf16 = jnp.bfloat16
f32 = jnp.float32

_Window = collections.namedtuple("_Window", "k v start width diag prev")


def _const_spec(shape):
    nd = len(shape)
    return pl.BlockSpec(shape, lambda *_: (0,) * nd, pipeline_mode=pl.Buffered(1))


def _adaln_kernel(ct_ref, w_ref, b_ref, *rest):
    n_cast = (len(rest) - 1) // 2
    cast_src, out_ref, cast_dst = rest[:n_cast], rest[n_cast], rest[n_cast + 1:]

    @pl.when(pl.program_id(0) == 0)
    def _():
        out_ref[...] = jnp.broadcast_to(b_ref[...], out_ref.shape)

    ct = ct_ref[...]
    act = ct / (1.0 + jnp.exp(-ct))
    w = w_ref[...]
    rows = []
    for b in range(ct.shape[1]):
        rows.append(jnp.sum(act[:, b:b + 1] * w, axis=0, keepdims=True))
    out_ref[...] += jnp.concatenate(rows, axis=0)

    for src, dst in zip(cast_src, cast_dst):
        dst[...] = src[...].astype(dst.dtype)


def _adaln(c, ada_w, ada_b, later_weights, tk=256):
    bsz, d = c.shape
    n = ada_w.shape[1]
    steps = d // tk

    def row_slice(w):
        rows, cols = w.shape
        assert rows % (steps * BF16_SUBLANES) == 0, w.shape
        return pl.BlockSpec((rows // steps, cols), lambda k: (k, 0))

    cast_specs = [row_slice(w) for w in later_weights]
    return pl.pallas_call(
        _adaln_kernel,
        grid=(steps,),
        in_specs=[
            pl.BlockSpec((tk, bsz), lambda k: (k, 0)),
            pl.BlockSpec((tk, n), lambda k: (k, 0)),
            pl.BlockSpec((1, n), lambda k: (0, 0)),
        ] + cast_specs,
        out_specs=[pl.BlockSpec((bsz, n), lambda k: (0, 0))] + cast_specs,
        out_shape=[jax.ShapeDtypeStruct((bsz, n), f32)]
        + [jax.ShapeDtypeStruct(w.shape, bf16) for w in later_weights],
        name="adaln",
        compiler_params=pltpu.CompilerParams(dimension_semantics=("arbitrary",)),
    )(c.T, ada_w, ada_b.reshape(1, n), *later_weights)


def _inproj_kernel(x_ref, mod_ref, g1_ref, w_in_ref, pool_w_ref, pool_b_ref, pool_s_ref,
                   qg_ref, kg_ref, *rest, d_pool, d_attn):
    n_cast = (len(rest) - 5) // 2
    cast_src = rest[:n_cast]
    yp_ref, q_ref, k_ref, v_ref = rest[n_cast:n_cast + 4]
    cast_dst = rest[n_cast + 4:2 * n_cast + 4]
    halo_ref = rest[-1]
    j = pl.program_id(1)
    tm = x_ref.shape[1]

    @pl.when(j == 0)
    def _():
        halo_ref[...] = jnp.zeros_like(halo_ref)

    x = x_ref[0]
    shift = mod_ref[0, 0:1, :]
    scale = mod_ref[0, 1:2, :]
    inv_rms = lax.rsqrt(jnp.mean(x * x, axis=-1, keepdims=True) + EPS)
    h = (x * inv_rms) * (g1_ref[...] * (1.0 + scale)) + shift
    hb = h.astype(bf16)

    project = lambda lo, hi: jnp.dot(hb, w_in_ref[:, lo:hi], preferred_element_type=f32)
    u = project(0, d_pool)
    tq = project(d_pool, d_pool + d_attn)
    tk = project(d_pool + d_attn, d_pool + 2 * d_attn)

    ext = jnp.concatenate([halo_ref[...], u], axis=0)
    halo_ref[...] = u[tm - POOL_HALO:, :]
    t1 = (j * tm + 1 + lax.broadcasted_iota(jnp.int32, (tm, 1), 0))
    ys = []
    for g, w in enumerate(POOL_WINDOWS):
        cols = slice(g * LANES, (g + 1) * LANES)
        s = ext[:, cols]
        span = 1
        while span < w:
            s = s + pltpu.roll(s, span, axis=0)
            span *= 2
        count = jnp.minimum(t1, w).astype(f32)
        pooled = s[POOL_HALO:, :] / count - u[:, cols]
        y = jnp.dot(pooled.astype(bf16), pool_w_ref[g], preferred_element_type=f32)
        ys.append((y + pool_b_ref[:, cols]) * pool_s_ref[:, cols])
    yp_ref[0] = jnp.concatenate(ys, axis=-1).astype(yp_ref.dtype)

    head0 = lax.broadcasted_iota(jnp.int32, (1, LANES), 1) < HEAD_DIM

    def head_norm(t, g_ref, post):
        gain = jnp.concatenate([g_ref[...], g_ref[...]], axis=-1) * post
        cols = []
        for c in range(d_attn // LANES):
            tc = t[:, c * LANES:(c + 1) * LANES]
            sq = tc * tc
            ssq0 = jnp.sum(jnp.where(head0, sq, 0.0), axis=-1, keepdims=True)
            ssq1 = jnp.sum(jnp.where(head0, 0.0, sq), axis=-1, keepdims=True)
            inv = jnp.where(head0, lax.rsqrt(ssq0 * (1.0 / HEAD_DIM) + EPS),
                            lax.rsqrt(ssq1 * (1.0 / HEAD_DIM) + EPS))
            cols.append((tc * inv) * gain)
        return jnp.concatenate(cols, axis=-1)

    q_ref[0] = head_norm(tq, qg_ref, -1.0 / math.sqrt(HEAD_DIM)).astype(q_ref.dtype)
    v_ref[0] = project(d_pool + 2 * d_attn, d_pool + 3 * d_attn).astype(v_ref.dtype)
    k_ref[0] = head_norm(tk, kg_ref, 1.0).astype(k_ref.dtype)

    for src, dst in zip(cast_src, cast_dst):
        dst[...] = src[...].astype(dst.dtype)


def _inproj(x, mod, g1, w_in, pool_w, pool_b, pool_s, qg, kg, later_weights, tm=1024):
    bsz, s, d = x.shape
    d_pool = pool_b.shape[-1]
    d_attn = (w_in.shape[1] - d_pool) // 3
    n_j = s // tm
    tile = lambda width: pl.BlockSpec((1, tm, width), lambda b, j: (b, j, 0))
    out = jax.ShapeDtypeStruct((bsz, s, d_attn), bf16)

    def row_slice(w):
        rows, cols = w.shape
        n_blocks = bsz * n_j
        while rows % (n_blocks * BF16_SUBLANES):
            assert n_blocks % 2 == 0, w.shape
            n_blocks //= 2
        return pl.BlockSpec((rows // n_blocks, cols), lambda b, j: (jnp.minimum(b * n_j + j, n_blocks - 1), 0))

    cast_specs = [row_slice(w) for w in later_weights]
    return pl.pallas_call(
        functools.partial(_inproj_kernel, d_pool=d_pool, d_attn=d_attn),
        grid=(bsz, n_j),
        in_specs=[
            tile(d),
            pl.BlockSpec((1,) + mod.shape[1:], lambda b, j: (b, 0, 0)),
            _const_spec(g1.shape),
            _const_spec(w_in.shape),
            _const_spec(pool_w.shape),
            _const_spec(pool_b.shape),
            _const_spec(pool_s.shape),
            _const_spec(qg.shape),
            _const_spec(kg.shape),
        ] + cast_specs,
        out_specs=[tile(d_pool), tile(d_attn), tile(d_attn), tile(d_attn)] + cast_specs,
        out_shape=[jax.ShapeDtypeStruct((bsz, s, d_pool), bf16), out, out, out]
        + [jax.ShapeDtypeStruct(w.shape, bf16) for w in later_weights],
        scratch_shapes=[pltpu.VMEM((POOL_HALO, d_pool), f32)],
        name="inproj",
        compiler_params=pltpu.CompilerParams(
            dimension_semantics=("arbitrary", "arbitrary"), vmem_limit_bytes=VMEM_LIMIT),
    )(x, mod, g1, w_in, pool_w, pool_b, pool_s, qg, kg, *later_weights)


def _attn_ffn_kernel(q_ref, kc_ref, vc_ref, kp_ref, vp_ref, k_hbm, v_hbm, og_ref, from_ref,
                     x_ref, yp_ref, mod_ref, g2_ref, w_out_ref, w_up_ref, cw_ref, cb_ref, w_down_ref,
                     out_ref,
                     o_scr, halo_ref, act_ref, run_ref, acc_ref, kbuf, vbuf, sem,
                     *, tiles_per_batch, d_pool, d_ff):
    s = pl.program_id(0)
    n_tiles = pl.num_programs(0) - 1
    ta = jnp.minimum(s, n_tiles - 1)
    ta_in_batch = lax.rem(ta, tiles_per_batch)
    batch_a = lax.div(ta, tiles_per_batch)
    tf_in_batch = lax.rem(jnp.maximum(s - 1, 0), tiles_per_batch)
    tm = q_ref.shape[1]
    n_qblk = tm // Q_BLOCK
    n_cols = q_ref.shape[2] // LANES

    @pl.when(s == 0)
    def _():
        o_scr[...] = jnp.zeros_like(o_scr)

    @pl.when(tf_in_batch == 0)
    def _():
        halo_ref[...] = jnp.zeros_like(halo_ref)

    lane = lax.broadcasted_iota(jnp.int32, (1, LANES), 1)
    head0 = lane < HEAD_DIM
    prev_valid = ta_in_batch > 0

    def rows(ref, start, width, c):
        cols = slice(c * LANES, (c + 1) * LANES)
        if len(ref.shape) == 3:
            return ref[0, start:start + width, cols]
        return ref[start:start + width, cols]

    def stacked_q(r, c):
        q = q_ref[0, r * Q_BLOCK:(r + 1) * Q_BLOCK, c * LANES:(c + 1) * LANES]
        zq = jnp.zeros_like(q)
        return jnp.concatenate([jnp.where(head0, q, zq), jnp.where(head0, zq, q)], axis=0)

    def scores(c, q2, win):
        kb = rows(win.k, win.start, win.width, c)
        return lax.dot_general(q2, kb, (((1,), (1,)), ((), ())), preferred_element_type=f32)

    def log_terms(zn, win):
        width = zn.shape[1]
        log_1m = jnp.minimum(zn, 0.0) - jnp.log(1.0 + jnp.exp2(jnp.abs(zn) * -LOG2_E))
        mask = None
        if win.diag:
            row = lax.broadcasted_iota(jnp.int32, zn.shape, 0) & (Q_BLOCK - 1)
            mask = lax.broadcasted_iota(jnp.int32, zn.shape, 1) < row
        elif win.prev:
            mask = prev_valid
        if mask is not None:
            log_1m = jnp.where(mask, log_1m, 0.0)
        from_here = jnp.dot(log_1m.astype(bf16), from_ref[0:width, 0:width], preferred_element_type=f32)
        return from_here - zn, jnp.sum(log_1m, axis=1, keepdims=True), mask

    def weighted_values(c, win, log_w, mask, run):
        a = jnp.exp(log_w if run is None else log_w + run)
        if mask is not None:
            a = jnp.where(mask, a, 0.0)
        return jnp.dot(a.astype(bf16), rows(win.v, win.start, win.width, c), preferred_element_type=f32)

    def visit(r, windows, fresh):
        halves = (range(0, n_cols // 2), range(n_cols // 2, n_cols))
        zs, terms = {}, {}
        for half in halves:
            for c in half:
                q2 = stacked_q(r, c)
                zs[c] = [scores(c, q2, w) for w in windows]
            yield
        for half in halves:
            for c in half:
                terms[c] = [log_terms(z, w) for z, w in zip(zs[c], windows)]
            yield
        for half in halves:
            for c in half:
                slot = r * n_cols + c
                run, acc = (None, None) if fresh else (run_ref[slot], acc_ref[slot])
                for w, (log_w, row_sum, mask) in zip(windows, terms[c]):
                    pv = weighted_values(c, w, log_w, mask, run)
                    acc = pv if acc is None else acc + pv
                    run = row_sum if run is None else run + row_sum
                run_ref[slot] = run
                acc_ref[slot] = acc
                if fresh:
                    normalise(r, c, acc)
            yield

    def normalise(r, c, acc):
        cols = slice(c * LANES, (c + 1) * LANES)
        o = jnp.where(head0, acc[0:Q_BLOCK, :], acc[Q_BLOCK:, :])
        o2 = o * o
        ssq0 = jnp.sum(jnp.where(head0, o2, 0.0), axis=-1, keepdims=True)
        ssq1 = jnp.sum(jnp.where(head0, 0.0, o2), axis=-1, keepdims=True)
        ms = jnp.where(head0, ssq0, ssq1) * (1.0 / HEAD_DIM)
        gain = jnp.concatenate([og_ref[2 * c:2 * c + 1, :], og_ref[2 * c + 1:2 * c + 2, :]], axis=-1)
        o_scr[r * Q_BLOCK:(r + 1) * Q_BLOCK, cols] = (o * lax.rsqrt(ms + EPS) * gain).astype(o_scr.dtype)

    def first_windows(r):
        wins = [_Window(kc_ref, vc_ref, r * Q_BLOCK, Q_BLOCK, True, False)]
        start = r * Q_BLOCK - K_WIDE
        if start >= 0:
            wins.append(_Window(kc_ref, vc_ref, start, K_WIDE, False, False))
        else:
            if r > 0:
                wins.append(_Window(kc_ref, vc_ref, 0, r * Q_BLOCK, False, False))
            wins.append(_Window(kp_ref, vp_ref, K_WIDE + start, -start, False, True))
        return wins

    def attention_main():
        blocks = [visit(r, first_windows(r), True) for r in range(n_qblk)]
        slots = sorted((2 * r + 2 * ATTN_SKEW * m + h, -r) for r in range(n_qblk) for m in range(3) for h in range(2))
        for _, neg_r in slots:
            next(blocks[-neg_r])
            yield

    gate1 = mod_ref[0, 2:3, :]
    shift = mod_ref[0, 3:4, :]
    scale = mod_ref[0, 4:5, :]
    gate2 = mod_ref[0, 5:6, :]
    n_chunks = d_ff // FF_CHUNK

    def ffn_main():
        mixed = (jnp.dot(yp_ref[0], w_out_ref[0:d_pool, :], preferred_element_type=f32)
                 + jnp.dot(o_scr[...], w_out_ref[d_pool:, :], preferred_element_type=f32))
        x1 = x_ref[0] + gate1 * mixed
        inv_rms = lax.rsqrt(jnp.mean(x1 * x1, axis=-1, keepdims=True) + EPS)
        hb = ((x1 * inv_rms) * (g2_ref[...] * (1.0 + scale)) + shift).astype(bf16)
        yield

        def conv(c, cols):
            up = jnp.dot(hb, w_up_ref[:, cols], preferred_element_type=f32)
            ext = jnp.concatenate([halo_ref[c], up], axis=0)
            halo_ref[c] = up[tm - CONV_HALO:, :]
            prev1 = pltpu.roll(ext, 1, axis=0)[CONV_HALO:, :]
            prev2 = pltpu.roll(ext, 2, axis=0)[CONV_HALO:, :]
            return (cw_ref[2:3, cols] * up + cw_ref[1:2, cols] * prev1
                    + cw_ref[0:1, cols] * prev2 + cb_ref[:, cols])

        for c in range(n_chunks):
            gate = conv(c, slice(c * FF_CHUNK, (c + 1) * FF_CHUNK))
            yield
            val = conv(n_chunks + c, slice(d_ff + c * FF_CHUNK, d_ff + (c + 1) * FF_CHUNK))
            act = gate / (1.0 + jnp.exp2(gate * -LOG2_E)) * val
            act_ref[:, c * FF_CHUNK:(c + 1) * FF_CHUNK] = act.astype(bf16)
            yield
        ffn = jnp.dot(act_ref[...], w_down_ref[...], preferred_element_type=f32)
        out_ref[0] = x1 + gate2 * ffn
        yield

    for _ in itertools.zip_longest(attention_main(), ffn_main()):
        pass

    def unfinished():
        return jnp.max(run_ref[...]) > LOG_WEIGHT_ZERO

    def drain(gen):
        for _ in gen:
            pass

    go = unfinished()

    @pl.when(go)
    def _():
        for r in range(1, n_qblk):
            wins = []
            seen = r * Q_BLOCK - K_WIDE
            if seen > 0:
                wins.append(_Window(kc_ref, vc_ref, 0, seen, False, False))
            top = K_WIDE + min(seen, 0)
            wins.append(_Window(kp_ref, vp_ref, 0, top, False, True))
            drain(visit(r, wins, False))

    def cond(state):
        jb, more = state
        return jnp.logical_and(jb >= 0, more)

    def body(state):
        jb, _ = state
        src = pl.ds(pl.multiple_of(jb * Q_BLOCK, Q_BLOCK), Q_BLOCK)
        copies = [pltpu.make_async_copy(k_hbm.at[batch_a, src, :], kbuf, sem.at[0]),
                  pltpu.make_async_copy(v_hbm.at[batch_a, src, :], vbuf, sem.at[1])]
        for cp in copies:
            cp.start()
        for cp in copies:
            cp.wait()
        for r in range(n_qblk):
            drain(visit(r, [_Window(kbuf, vbuf, 0, Q_BLOCK, False, False)], False))
        return jb - 1, unfinished()

    first = ta_in_batch * n_qblk - (K_WIDE // Q_BLOCK + 1)
    lax.while_loop(cond, body, (first, go))

    @pl.when(go)
    def _():
        for r in range(n_qblk):
            for c in range(n_cols):
                normalise(r, c, acc_ref[r * n_cols + c])


def _attn_ffn(x, yp, q, k, v, mod, out_g, g2, w_out, w_up, conv_w, conv_b, w_down, tm=512):
    bsz, s, d = x.shape
    d_pool = yp.shape[-1]
    d_attn = q.shape[-1]
    d_ff = w_down.shape[0]
    tpb = s // tm
    n_tiles = bsz * tpb
    n_slots = (tm // Q_BLOCK) * (d_attn // LANES)
    idx = lax.broadcasted_iota(jnp.int32, (K_WIDE, K_WIDE), 0)
    from_here = (idx >= idx.T).astype(bf16)

    def attn_tile(step):
        t = jnp.minimum(step, n_tiles - 1)
        return t // tpb, t % tpb

    def ffn_tile(step):
        t = jnp.maximum(step - 1, 0)
        return t // tpb, t % tpb

    def a_spec(width):
        return pl.BlockSpec((1, tm, width), lambda i: (*attn_tile(i), 0))

    def prev_spec(width):
        per_tile = tm // K_WIDE
        return pl.BlockSpec((1, K_WIDE, width),
                            lambda i: (attn_tile(i)[0], jnp.maximum(attn_tile(i)[1] * per_tile - 1, 0), 0))

    def f_spec(width):
        return pl.BlockSpec((1, tm, width), lambda i: (*ffn_tile(i), 0))

    hbm = pl.BlockSpec(memory_space=pl.ANY)
    return pl.pallas_call(
        functools.partial(_attn_ffn_kernel, tiles_per_batch=tpb, d_pool=d_pool, d_ff=d_ff),
        grid=(n_tiles + 1,),
        in_specs=[
            a_spec(d_attn), a_spec(d_attn), a_spec(d_attn), prev_spec(d_attn), prev_spec(d_attn), hbm, hbm,
            _const_spec(out_g.shape), _const_spec(from_here.shape),
            f_spec(d), f_spec(d_pool),
            pl.BlockSpec((1,) + mod.shape[1:], lambda i: (ffn_tile(i)[0], 0, 0)),
            _const_spec(g2.shape),
            _const_spec(w_out.shape),
            _const_spec(w_up.shape),
            _const_spec(conv_w.shape),
            _const_spec(conv_b.shape),
            _const_spec(w_down.shape),
        ],
        out_specs=f_spec(d),
        out_shape=jax.ShapeDtypeStruct((bsz, s, d), x.dtype),
        scratch_shapes=[
            pltpu.VMEM((tm, d_attn), bf16),
            pltpu.VMEM((2 * d_ff // FF_CHUNK, CONV_HALO, FF_CHUNK), f32),
            pltpu.VMEM((tm, d_ff), bf16),
            pltpu.VMEM((n_slots, 2 * Q_BLOCK, 1), f32),
            pltpu.VMEM((n_slots, 2 * Q_BLOCK, LANES), f32),
            pltpu.VMEM((Q_BLOCK, d_attn), bf16),
            pltpu.VMEM((Q_BLOCK, d_attn), bf16),
            pltpu.SemaphoreType.DMA((2,)),
        ],
        name="attn_convffn",
        compiler_params=pltpu.CompilerParams(
            dimension_semantics=("arbitrary",), vmem_limit_bytes=FUSED_VMEM_LIMIT),
    )(q, k, v, k, v, k, v, out_g, from_here, x, yp, mod, g2, w_out, w_up, conv_w, conv_b, w_down)


def kernel(x, c, ada_w, ada_b, norm1_g, w_in, pool_w, pool_b, pool_scale, q_norm_g, k_norm_g, attn_out_g,
           w_out, norm2_g, w_up, conv_w, conv_b, w_down):
    depth = ada_w.shape[0]
    d = x.shape[-1]
    row = lambda a: a.reshape(1, -1)
    for l in range(depth):
        mod, w_in_b, pool_w_b = _adaln(c, ada_w[l], ada_b[l], (w_in[l], pool_w[l].reshape(-1, pool_w.shape[-1])))
        mod = mod.reshape(c.shape[0], -1, d)
        yp, q, k, v, w_out_b, w_up_b, w_down_b = _inproj(
            x, mod, row(norm1_g[l]), w_in_b, pool_w_b.reshape(pool_w[l].shape), row(pool_b[l]),
            row(pool_scale[l]), row(q_norm_g[l]), row(k_norm_g[l]),
            (w_out[l], w_up[l], w_down[l]))
        x = _attn_ffn(x, yp, q, k, v, mod, attn_out_g[l], row(norm2_g[l]), w_out_b, w_up_b,
                      conv_w[l], row(conv_b[l]), w_down_b)
    return x
```
